```python
import math
import jax, jax.numpy as jnp
from jax import lax
import numpy as np

D_MODEL = 2048
BATCH = 8
SEQ = 2048
DEPTH = 2

GATE_WIDTH = D_MODEL
SB_WIDTH = D_MODEL // 2
SB_HEAD_DIM = 128
SB_HEADS = SB_WIDTH // SB_HEAD_DIM
SB_BLOCK = 128
POOL_WIDTH = D_MODEL - SB_WIDTH
POOL_WINDOWS = (2, 4, 8, 16)
POOL_GROUPS = len(POOL_WINDOWS)
POOL_GROUP_DIM = POOL_WIDTH // POOL_GROUPS
EVEN_IN = 3 * SB_WIDTH + POOL_WIDTH + GATE_WIDTH
SCONV_WIDTH = D_MODEL // 2
SCONV_K = 3
CONF_WIDTH = D_MODEL - SCONV_WIDTH
CONF_K = 31
ODD_IN = 3 * SCONV_WIDTH + 2 * CONF_WIDTH + GATE_WIDTH
N_EVEN = (DEPTH + 1) // 2
N_ODD = DEPTH // 2
EPS = 1e-6

kernel_name = "hybrid_stickbreak_pool_shortconv_conformer"


def rms_norm(x, g):
    xf = x.astype(jnp.float32)
    y = xf * lax.rsqrt(jnp.mean(xf * xf, axis=-1, keepdims=True) + EPS)
    return (y * g.astype(jnp.float32)).astype(x.dtype)


def layer_norm(x, g, b):
    xf = x.astype(jnp.float32)
    mu = jnp.mean(xf, axis=-1, keepdims=True)
    var = jnp.mean(jnp.square(xf - mu), axis=-1, keepdims=True)
    y = (xf - mu) * lax.rsqrt(var + EPS)
    return (y * g.astype(jnp.float32) + b.astype(jnp.float32)).astype(x.dtype)


def causal_depthwise_conv(x, w):
    k, c = w.shape
    return lax.conv_general_dilated(
        x, w.astype(x.dtype)[:, None, :], window_strides=(1,), padding=[(k - 1, 0)],
        dimension_numbers=("NWC", "WIO", "NWC"), feature_group_count=c)


def stick_breaking_attention(q, k, v):
    b, s_len, h, dh = q.shape
    scale = 1.0 / math.sqrt(dh)
    outs = []
    for qb in range(s_len // SB_BLOCK):
        q0 = qb * SB_BLOCK
        kend = q0 + SB_BLOCK
        z = jnp.einsum("bqhd,bkhd->bhqk", q[:, q0:kend], k[:, :kend]).astype(jnp.float32) * scale
        t_idx = q0 + jnp.arange(SB_BLOCK)[:, None]
        s_idx = jnp.arange(kend)[None, :]
        mask = s_idx < t_idx
        log_beta = jax.nn.log_sigmoid(z)
        log_1m = jnp.where(mask, jax.nn.log_sigmoid(-z), 0.0)
        log_stay = lax.cumsum(log_1m, axis=3, reverse=True) - log_1m
        wts = jnp.where(mask, jnp.exp(log_beta + log_stay), 0.0)
        outs.append(jnp.einsum("bhqk,bkhd->bqhd", wts.astype(v.dtype), v[:, :kend]))
    return jnp.concatenate(outs, axis=1)


def multiscale_pool(u, pool_w, pool_scale):
    b, s_len, _ = u.shape
    ug = u.reshape(b, s_len, POOL_GROUPS, POOL_GROUP_DIM)
    cs = jnp.cumsum(ug.astype(jnp.float32), axis=1)
    pos1 = jnp.arange(1, s_len + 1)
    pooled = []
    for gi, win in enumerate(POOL_WINDOWS):
        c = cs[:, :, gi]
        prev = jnp.pad(c, ((0, 0), (win, 0), (0, 0)))[:, :s_len]
        count = jnp.minimum(win, pos1).astype(jnp.float32)[None, :, None]
        pooled.append(((c - prev) / count).astype(u.dtype) - ug[:, :, gi])
    pooled = jnp.stack(pooled, axis=2)
    y = jnp.einsum("bsgc,gcd->bsgd", pooled, pool_w).reshape(b, s_len, POOL_WIDTH)
    return y * pool_scale


def even_mixer(h, w_in, pool_w, pool_scale, w_out):
    b, s_len, _ = h.shape
    p = h @ w_in
    q, k, v, u, g = jnp.split(p, np.cumsum([SB_WIDTH, SB_WIDTH, SB_WIDTH, POOL_WIDTH]).tolist(), axis=-1)
    hs = (b, s_len, SB_HEADS, SB_HEAD_DIM)
    a = stick_breaking_attention(q.reshape(hs), k.reshape(hs), v.reshape(hs)).reshape(b, s_len, SB_WIDTH)
    po = multiscale_pool(u, pool_w, pool_scale)
    y = jnp.concatenate([a, po], axis=-1) * jax.nn.silu(g)
    return y @ w_out


def odd_mixer(h, w_in, sconv_w, dconv_w, dconv_b, cnorm_g, cnorm_b, w_out):
    p = h @ w_in
    hc, bc, cc, ga, gb, g = jnp.split(
        p, np.cumsum([SCONV_WIDTH, SCONV_WIDTH, SCONV_WIDTH, CONF_WIDTH, CONF_WIDTH]).tolist(), axis=-1)
    c_out = bc * causal_depthwise_conv(cc * hc, sconv_w)
    d = ga * jax.nn.sigmoid(gb)
    d = causal_depthwise_conv(d, dconv_w) + dconv_b
    d = jax.nn.silu(layer_norm(d, cnorm_g, cnorm_b))
    y = jnp.concatenate([c_out, d], axis=-1) * jax.nn.silu(g)
    return y @ w_out


def setup_inputs(seed: int = 0) -> dict:
    key = jax.random.key(seed)
    ks = jax.random.split(key, 20)
    f32 = jnp.float32
    nrm = lambda k, shape, s: jax.random.normal(k, shape, f32) * s
    return {
        "x": jax.random.normal(ks[0], (BATCH, SEQ, D_MODEL), f32),
        "ln_pre_even": 1.0 + nrm(ks[1], (N_EVEN, D_MODEL), 0.05),
        "w_in_even": nrm(ks[2], (N_EVEN, D_MODEL, EVEN_IN), D_MODEL ** -0.5),
        "pool_w": nrm(ks[3], (N_EVEN, POOL_GROUPS, POOL_GROUP_DIM, POOL_GROUP_DIM), POOL_GROUP_DIM ** -0.5),
        "pool_scale": 1.0 + nrm(ks[4], (N_EVEN, POOL_WIDTH), 0.1),
        "w_out_even": nrm(ks[5], (N_EVEN, D_MODEL, D_MODEL), D_MODEL ** -0.5),
        "ln_post_even": 1.0 + nrm(ks[6], (N_EVEN, D_MODEL), 0.05),
        "ln_pre_odd": 1.0 + nrm(ks[7], (N_ODD, D_MODEL), 0.05),
        "w_in_odd": nrm(ks[8], (N_ODD, D_MODEL, ODD_IN), D_MODEL ** -0.5),
        "sconv_w": nrm(ks[9], (N_ODD, SCONV_K, SCONV_WIDTH), SCONV_K ** -0.5),
        "dconv_w": nrm(ks[10], (N_ODD, CONF_K, CONF_WIDTH), CONF_K ** -0.5),
        "dconv_b": nrm(ks[11], (N_ODD, CONF_WIDTH), 0.02),
        "cnorm_g": 1.0 + nrm(ks[12], (N_ODD, CONF_WIDTH), 0.05),
        "cnorm_b": nrm(ks[13], (N_ODD, CONF_WIDTH), 0.02),
        "w_out_odd": nrm(ks[14], (N_ODD, D_MODEL, D_MODEL), D_MODEL ** -0.5),
        "ln_post_odd": 1.0 + nrm(ks[15], (N_ODD, D_MODEL), 0.05),
    }


def reference(x, ln_pre_even, w_in_even, pool_w, pool_scale, w_out_even, ln_post_even,
              ln_pre_odd, w_in_odd, sconv_w, dconv_w, dconv_b, cnorm_g, cnorm_b, w_out_odd, ln_post_odd):
    for layer in range(DEPTH):
        i = layer // 2
        if layer % 2 == 0:
            h = rms_norm(x, ln_pre_even[i])
            o = even_mixer(h, w_in_even[i], pool_w[i], pool_scale[i], w_out_even[i])
            x = x + rms_norm(o, ln_post_even[i])
        else:
            h = rms_norm(x, ln_pre_odd[i])
            o = odd_mixer(h, w_in_odd[i], sconv_w[i], dconv_w[i], dconv_b[i], cnorm_g[i], cnorm_b[i], w_out_odd[i])
            x = x + rms_norm(o, ln_post_odd[i])
    return x
```

```python
import functools
import math

import jax
import jax.numpy as jnp
from jax import lax
from jax.experimental import pallas as pl
from jax.experimental.pallas import tpu as pltpu

F32 = jnp.float32
BF16 = jnp.bfloat16
EPS = 1e-6

SB_HEAD_DIM = 128
POOL_WINDOWS = (2, 4, 8, 16)
SCONV_K = 3
CONF_K = 31

VMEM_LIMIT_BYTES = 56 * 1024 * 1024

INPROJ_TM = 1024
INPROJ_TN = 1024
ATTN_BLOCK = 256
TAIL_TM = 256
POOL_HALO = 16
CONV_HALO = 32
CONV_ROWS = 128
LANES = 128


def _sigmoid(v):
    return 1.0 / (1.0 + jnp.exp(-v))


def _silu(v):
    return v * _sigmoid(v)


def _norm_inproj_kernel(x_ref, g_ref, w_ref, o_ref, h_ref):
    @pl.when(pl.program_id(1) == 0)
    def _():
        x = x_ref[...]
        ms = jnp.mean(x * x, axis=-1, keepdims=True)
        h_ref[...] = (x * lax.rsqrt(ms + EPS) * g_ref[...]).astype(BF16)

    o_ref[...] = jnp.dot(h_ref[...], w_ref[...], preferred_element_type=F32).astype(o_ref.dtype)


def _norm_inproj(x2d, gain, w):
    m, d = x2d.shape
    n = w.shape[1]
    tm, tn = INPROJ_TM, INPROJ_TN
    assert m % tm == 0 and n % tn == 0
    return pl.pallas_call(
        _norm_inproj_kernel,
        grid=(m // tm, n // tn),
        in_specs=[
            pl.BlockSpec((tm, d), lambda i, j: (i, 0)),
            pl.BlockSpec((1, d), lambda i, j: (0, 0)),
            pl.BlockSpec((d, tn), lambda i, j: (0, j)),
        ],
        out_specs=pl.BlockSpec((tm, tn), lambda i, j: (i, j)),
        out_shape=jax.ShapeDtypeStruct((m, n), BF16),
        scratch_shapes=[pltpu.VMEM((tm, d), BF16)],
        compiler_params=pltpu.CompilerParams(
            dimension_semantics=("parallel", "arbitrary"), vmem_limit_bytes=VMEM_LIMIT_BYTES),
        name="norm_inproj",
    )(x2d, gain.reshape(1, d), w)


def _attn_kernel(q_ref, k_ref, v_ref, o_ref, acc_ref, run_ref, *, seq, blk):
    row = lax.broadcasted_iota(jnp.int32, (blk, blk), 0)
    col = lax.broadcasted_iota(jnp.int32, (blk, blk), 1)
    later_keys = jnp.where(row > col, 1.0, 0.0).astype(BF16)
    causal = col < row

    def block(q, kb, vb, masked):
        z = lax.dot_general(q, kb, (((1,), (1,)), ((), ())), preferred_element_type=F32)
        sp = jnp.maximum(z, 0.0) + jnp.log(1.0 + jnp.exp(-jnp.abs(z)))
        spm = jnp.where(causal, sp, 0.0) if masked else sp
        hi = spm.astype(BF16)
        lo = (spm - hi.astype(F32)).astype(BF16)
        later = (jnp.dot(hi, later_keys, preferred_element_type=F32)
                 + jnp.dot(lo, later_keys, preferred_element_type=F32))
        run = run_ref[...]
        arg = z - sp - later - jnp.concatenate([run] * (blk // LANES), axis=1)
        w = jnp.exp(arg)
        if masked:
            w = jnp.where(causal, w, 0.0)
        acc_ref[...] += jnp.dot(w.astype(BF16), vb, preferred_element_type=F32)
        run_ref[...] = run + jnp.sum(spm, axis=-1, keepdims=True)

    def q_body(qi, carry):
        q0 = pl.multiple_of(qi * blk, blk)
        q = q_ref[pl.ds(q0, blk), :]
        acc_ref[...] = jnp.zeros_like(acc_ref)
        run_ref[...] = jnp.zeros_like(run_ref)
        block(q, k_ref[pl.ds(q0, blk), :], v_ref[pl.ds(q0, blk), :], True)

        def k_body(n, c):
            k0 = pl.multiple_of((qi - 1 - n) * blk, blk)
            block(q, k_ref[pl.ds(k0, blk), :], v_ref[pl.ds(k0, blk), :], False)
            return c

        lax.fori_loop(0, qi, k_body, 0)
        o_ref[pl.ds(q0, blk), :] = acc_ref[...].astype(o_ref.dtype)
        return carry

    lax.fori_loop(0, seq // blk, q_body, 0)


def _attention(p, batch, seq, heads):
    blk = ATTN_BLOCK
    assert seq % blk == 0
    dh = SB_HEAD_DIM
    return pl.pallas_call(
        functools.partial(_attn_kernel, seq=seq, blk=blk),
        grid=(batch, heads),
        in_specs=[
            pl.BlockSpec((seq, dh), lambda b, h: (b, h)),
            pl.BlockSpec((seq, dh), lambda b, h: (b, heads + h)),
            pl.BlockSpec((seq, dh), lambda b, h: (b, 2 * heads + h)),
        ],
        out_specs=pl.BlockSpec((seq, dh), lambda b, h: (b, h)),
        out_shape=jax.ShapeDtypeStruct((batch * seq, heads * dh), BF16),
        scratch_shapes=[pltpu.VMEM((blk, dh), F32), pltpu.VMEM((blk, LANES), F32)],
        compiler_params=pltpu.CompilerParams(
            dimension_semantics=("parallel", "parallel"), vmem_limit_bytes=VMEM_LIMIT_BYTES),
        name="stickbreak_attention",
    )(p, p, p)


def _outproj_residual(y_ref, wo_ref, ln_ref, x_ref, o_ref):
    o = jnp.dot(y_ref[...], wo_ref[...], preferred_element_type=F32)
    ms = jnp.mean(o * o, axis=-1, keepdims=True)
    o_ref[...] = x_ref[...] + o * lax.rsqrt(ms + EPS) * ln_ref[...]


def _even_tail_kernel(a_ref, u_ref, uh_ref, g_ref, x_ref, pw_ref, ps_ref, wo_ref, ln_ref, o_ref, y_ref,
                      *, seq, tm, sb_width, group_dim):
    i = pl.program_id(0)
    pos0 = (i * tm) % seq
    halo = POOL_HALO
    u_hist = jnp.where(pos0 == 0, jnp.zeros((halo, u_ref.shape[1]), BF16), uh_ref[...])
    u = u_ref[...]

    t_idx = lax.broadcasted_iota(jnp.int32, (tm, tm), 0)
    j_idx = lax.broadcasted_iota(jnp.int32, (tm, tm), 1)
    lag = t_idx - j_idx
    th_idx = lax.broadcasted_iota(jnp.int32, (halo, halo), 0)
    jh_idx = lax.broadcasted_iota(jnp.int32, (halo, halo), 1)
    lag_hist = th_idx + halo - jh_idx
    pos = pos0 + lax.broadcasted_iota(jnp.int32, (tm, 1), 0)

    y_ref[:, :sb_width] = (a_ref[...].astype(F32) * _silu(g_ref[:, :sb_width].astype(F32))).astype(BF16)

    for gi, win in enumerate(POOL_WINDOWS):
        lo_c, hi_c = gi * group_dim, (gi + 1) * group_dim
        ug = u[:, lo_c:hi_c]
        band = jnp.where((lag >= 0) & (lag < win), 1.0, 0.0).astype(BF16)
        wsum = jnp.dot(band, ug, preferred_element_type=F32)
        band_hist = jnp.where(lag_hist < win, 1.0, 0.0).astype(BF16)
        head = wsum[:halo] + jnp.dot(band_hist, u_hist[:, lo_c:hi_c], preferred_element_type=F32)
        wsum = jnp.concatenate([head, wsum[halo:]], axis=0)
        inv_count = 1.0 / jnp.minimum(win, pos + 1).astype(F32)
        pooled = wsum * inv_count - ug.astype(F32)
        yg = jnp.dot(pooled.astype(BF16), pw_ref[gi], preferred_element_type=F32) * ps_ref[:, lo_c:hi_c]
        gate = _silu(g_ref[:, sb_width + lo_c:sb_width + hi_c].astype(F32))
        y_ref[:, sb_width + lo_c:sb_width + hi_c] = (yg * gate).astype(BF16)

    _outproj_residual(y_ref, wo_ref, ln_ref, x_ref, o_ref)


def _even_tail(a, p, x2d, pool_w, pool_scale, w_out, ln_post, seq, sb_width):
    m, d = x2d.shape
    tm, halo = TAIL_TM, POOL_HALO
    pool_width = d - sb_width
    groups, group_dim, _ = pool_w.shape
    assert seq % tm == 0 and tm % halo == 0 and groups == len(POOL_WINDOWS)
    assert pool_width == sb_width and d % pool_width == 0
    u_col = (3 * sb_width) // pool_width
    g_col = (3 * sb_width + pool_width) // d
    assert u_col * pool_width == 3 * sb_width and g_col * d == 3 * sb_width + pool_width
    const = lambda i: (0, 0)
    return pl.pallas_call(
        functools.partial(_even_tail_kernel, seq=seq, tm=tm, sb_width=sb_width, group_dim=group_dim),
        grid=(m // tm,),
        in_specs=[
            pl.BlockSpec((tm, sb_width), lambda i: (i, 0)),
            pl.BlockSpec((tm, pool_width), lambda i: (i, u_col)),
            pl.BlockSpec((halo, pool_width), lambda i: (jnp.maximum(i * (tm // halo) - 1, 0), u_col)),
            pl.BlockSpec((tm, d), lambda i: (i, g_col)),
            pl.BlockSpec((tm, d), lambda i: (i, 0)),
            pl.BlockSpec((groups, group_dim, group_dim), lambda i: (0, 0, 0)),
            pl.BlockSpec((1, pool_width), const),
            pl.BlockSpec((d, d), const),
            pl.BlockSpec((1, d), const),
        ],
        out_specs=pl.BlockSpec((tm, d), lambda i: (i, 0)),
        out_shape=jax.ShapeDtypeStruct((m, d), F32),
        scratch_shapes=[pltpu.VMEM((tm, d), BF16)],
        compiler_params=pltpu.CompilerParams(
            dimension_semantics=("parallel",), vmem_limit_bytes=VMEM_LIMIT_BYTES),
        name="even_tail",
    )(a, p, p, p, x2d, pool_w, pool_scale.reshape(1, pool_width), w_out, ln_post.reshape(1, d))


def _odd_tail_kernel(p_ref, ph_ref, x_ref, sw_ref, dw_ref, db_ref, cg_ref, cb_ref, wo_ref, ln_ref, o_ref,
                     zbuf, dbuf, cbuf, y_ref, *, seq, tm, width):
    i = pl.program_id(0)
    halo = CONV_HALO
    w = width
    keep = jnp.where((i * tm) % seq == 0, 0.0, 1.0).astype(F32)

    def piece(ref, n):
        return ref[:, n * w:(n + 1) * w].astype(F32)

    zbuf[:halo, :] = piece(ph_ref, 2) * piece(ph_ref, 0) * keep
    zbuf[halo:, :] = piece(p_ref, 2) * piece(p_ref, 0)
    dbuf[:halo, :] = piece(ph_ref, 3) * _sigmoid(piece(ph_ref, 4)) * keep
    dbuf[halo:, :] = piece(p_ref, 3) * _sigmoid(piece(p_ref, 4))

    c3 = jnp.zeros((tm, w), F32)
    for k in range(SCONV_K):
        off = halo - (SCONV_K - 1) + k
        c3 = c3 + sw_ref[k:k + 1, :] * zbuf[off:off + tm, :]
    gate_c = _silu(p_ref[:, 5 * w:6 * w].astype(F32))
    y_ref[:, :w] = (piece(p_ref, 1) * c3 * gate_c).astype(BF16)

    for r0 in range(0, tm, CONV_ROWS):
        for c0 in range(0, w, LANES):
            acc = jnp.broadcast_to(db_ref[:, c0:c0 + LANES], (CONV_ROWS, LANES))
            for k in range(CONF_K):
                off = halo - (CONF_K - 1) + k + r0
                acc = acc + dw_ref[k:k + 1, c0:c0 + LANES] * dbuf[off:off + CONV_ROWS, c0:c0 + LANES]
            cbuf[r0:r0 + CONV_ROWS, c0:c0 + LANES] = acc

    d = cbuf[...]
    mu = jnp.mean(d, axis=-1, keepdims=True)
    dc = d - mu
    var = jnp.mean(dc * dc, axis=-1, keepdims=True)
    dn = _silu(dc * lax.rsqrt(var + EPS) * cg_ref[...] + cb_ref[...])
    gate_d = _silu(p_ref[:, 6 * w:7 * w].astype(F32))
    y_ref[:, w:] = (dn * gate_d).astype(BF16)

    _outproj_residual(y_ref, wo_ref, ln_ref, x_ref, o_ref)


def _odd_tail(p, x2d, sconv_w, dconv_w, dconv_b, cnorm_g, cnorm_b, w_out, ln_post, seq, width):
    m, d = x2d.shape
    n_in = p.shape[1]
    tm, halo = TAIL_TM, CONV_HALO
    assert seq % tm == 0 and tm % halo == 0 and halo >= CONF_K - 1 and d == 2 * width and n_in == 7 * width
    assert sconv_w.shape == (SCONV_K, width) and dconv_w.shape == (CONF_K, width)
    const = lambda i: (0, 0)
    return pl.pallas_call(
        functools.partial(_odd_tail_kernel, seq=seq, tm=tm, width=width),
        grid=(m // tm,),
        in_specs=[
            pl.BlockSpec((tm, n_in), lambda i: (i, 0)),
            pl.BlockSpec((halo, n_in), lambda i: (jnp.maximum(i * (tm // halo) - 1, 0), 0)),
            pl.BlockSpec((tm, d), lambda i: (i, 0)),
            pl.BlockSpec((SCONV_K, width), const),
            pl.BlockSpec((CONF_K, width), const),
            pl.BlockSpec((1, width), const),
            pl.BlockSpec((1, width), const),
            pl.BlockSpec((1, width), const),
            pl.BlockSpec((d, d), const),
            pl.BlockSpec((1, d), const),
        ],
        out_specs=pl.BlockSpec((tm, d), lambda i: (i, 0)),
        out_shape=jax.ShapeDtypeStruct((m, d), F32),
        scratch_shapes=[
            pltpu.VMEM((halo + tm, width), F32),
            pltpu.VMEM((halo + tm, width), F32),
            pltpu.VMEM((tm, width), F32),
            pltpu.VMEM((tm, d), BF16),
        ],
        compiler_params=pltpu.CompilerParams(
            dimension_semantics=("parallel",), vmem_limit_bytes=VMEM_LIMIT_BYTES),
        name="odd_tail",
    )(p, p, x2d, sconv_w, dconv_w, dconv_b.reshape(1, width), cnorm_g.reshape(1, width),
      cnorm_b.reshape(1, width), w_out, ln_post.reshape(1, d))


def kernel(x, ln_pre_even, w_in_even, pool_w, pool_scale, w_out_even, ln_post_even,
           ln_pre_odd, w_in_odd, sconv_w, dconv_w, dconv_b, cnorm_g, cnorm_b, w_out_odd, ln_post_odd):
    batch, seq, d = x.shape
    depth = ln_pre_even.shape[0] + ln_pre_odd.shape[0]
    sb_width = d // 2
    heads = sb_width // SB_HEAD_DIM
    conv_width = d // 2
    scale = 1.0 / math.sqrt(SB_HEAD_DIM)

    xf = x.reshape(batch * seq, d)
    for layer in range(depth):
        i = layer // 2
        if layer % 2 == 0:
            w_in = w_in_even[i]
            w_in = jnp.concatenate([w_in[:, :sb_width] * scale, w_in[:, sb_width:]], axis=1).astype(BF16)
            p = _norm_inproj(xf, ln_pre_even[i], w_in)
            a = _attention(p, batch, seq, heads)
            xf = _even_tail(a, p, xf, pool_w[i].astype(BF16), pool_scale[i], w_out_even[i].astype(BF16),
                            ln_post_even[i], seq, sb_width)
        else:
            p = _norm_inproj(xf, ln_pre_odd[i], w_in_odd[i].astype(BF16))
            xf = _odd_tail(p, xf, sconv_w[i], dconv_w[i], dconv_b[i], cnorm_g[i], cnorm_b[i],
                           w_out_odd[i].astype(BF16), ln_post_odd[i], seq, conv_width)
    return xf.reshape(batch, seq, d)
```

```python
import functools
import math

import jax
import jax.numpy as jnp
from jax import lax
from jax.experimental import pallas as pl
from jax.experimental.pallas import tpu as pltpu

F32 = jnp.float32
BF16 = jnp.bfloat16
EPS = 1e-6

SB_HEAD_DIM = 128
POOL_WINDOWS = (2, 4, 8, 16)
SCONV_K = 3
CONF_K = 31

VMEM_LIMIT_BYTES = 56 * 1024 * 1024
LANES = 128
MXU_DIM = 256

INPROJ_TM = 1024
INPROJ_TN_MAX = 2048
ATTN_BLOCK = 256
EVEN_TM = 512
ODD_TM = 512
POOL_CHUNK = 256
POOL_HALO = 16
CONV_HALO = 32
CONV_ROWS = 128
OUT_ROWS = 256


def _sigmoid(v):
    return 1.0 / (1.0 + jnp.exp(-v))


def _silu(v):
    return v * _sigmoid(v)


def _resident(block_shape, index_map):
    return pl.BlockSpec(block_shape, index_map, pipeline_mode=pl.Buffered(1))


def _norm_inproj_kernel(x_ref, g_ref, w_ref, *rest):
    *maybe_scale_ref, o_ref, h_ref = rest

    @pl.when(pl.program_id(1) == 0)
    def _():
        x = x_ref[...]
        ms = jnp.mean(x * x, axis=-1, keepdims=True)
        h_ref[...] = (x * lax.rsqrt(ms + EPS) * g_ref[...]).astype(BF16)

    o = jnp.dot(h_ref[...], w_ref[...], preferred_element_type=F32)
    if maybe_scale_ref:
        o = o * maybe_scale_ref[0][...]
    o_ref[...] = o.astype(o_ref.dtype)


def _norm_inproj(x2d, gain, w, col_scale=None):
    m, d = x2d.shape
    n = w.shape[1]
    tm = INPROJ_TM
    tn = max(t for t in range(MXU_DIM, INPROJ_TN_MAX + 1, MXU_DIM) if n % t == 0)
    assert m % tm == 0
    in_specs = [
        pl.BlockSpec((tm, d), lambda i, j: (i, 0)),
        _resident((1, d), lambda i, j: (0, 0)),
        pl.BlockSpec((d, tn), lambda i, j: (0, j)),
    ]
    operands = [x2d, gain.reshape(1, d), w]
    if col_scale is not None:
        in_specs.append(pl.BlockSpec((1, tn), lambda i, j: (0, j)))
        operands.append(col_scale.reshape(1, n))
    return pl.pallas_call(
        _norm_inproj_kernel,
        grid=(m // tm, n // tn),
        in_specs=in_specs,
        out_specs=pl.BlockSpec((tm, tn), lambda i, j: (i, j)),
        out_shape=jax.ShapeDtypeStruct((m, n), BF16),
        scratch_shapes=[pltpu.VMEM((tm, d), BF16)],
        compiler_params=pltpu.CompilerParams(
            dimension_semantics=("parallel", "arbitrary"), vmem_limit_bytes=VMEM_LIMIT_BYTES),
        name="norm_inproj",
    )(*operands)


def _attn_kernel(q_ref, k_ref, v_ref, o_ref, acc_ref, run_ref, *, seq, blk, heads):
    dh = SB_HEAD_DIM
    row = lax.broadcasted_iota(jnp.int32, (blk, blk), 0)
    col = lax.broadcasted_iota(jnp.int32, (blk, blk), 1)
    later_keys = jnp.where(row > col, 1.0, 0.0).astype(BF16)
    causal = col < row

    def blocks(q0, k0s, masked):
        log_beta, stay_terms = [], []
        for k0 in k0s:
            for h in range(heads):
                lanes = slice(h * dh, (h + 1) * dh)
                q = q_ref[pl.ds(q0, blk), lanes]
                kb = k_ref[pl.ds(k0, blk), lanes]
                z = lax.dot_general(q, kb, (((1,), (1,)), ((), ())), preferred_element_type=F32)
                pos = jnp.maximum(z, 0.0)
                neg = jnp.minimum(z, 0.0)
                tail = jnp.log2(1.0 + jnp.exp2(neg - pos))
                log_beta.append(neg - tail)
                sp = pos + tail
                stay_terms.append(jnp.where(causal, sp, 0.0) if masked else sp)
        stacked = jnp.concatenate([s.astype(BF16) for s in stay_terms], axis=0)
        later = jnp.dot(stacked, later_keys, preferred_element_type=F32)
        for h in range(heads):
            lanes = slice(h * dh, (h + 1) * dh)
            run = run_ref[:, lanes]
            out = None
            for n, k0 in enumerate(k0s):
                i = n * heads + h
                arg = log_beta[i] - later[i * blk:(i + 1) * blk] - jnp.concatenate([run] * (blk // LANES), axis=1)
                w = jnp.exp2(arg)
                if masked:
                    w = jnp.where(causal, w, 0.0)
                pv = jnp.dot(w.astype(BF16), v_ref[pl.ds(k0, blk), lanes], preferred_element_type=F32)
                out = pv if out is None else out + pv
                run = run + jnp.sum(stay_terms[i], axis=-1, keepdims=True)
            acc_ref[:, lanes] += out
            run_ref[:, lanes] = run

    def q_body(qi, carry):
        q0 = pl.multiple_of(qi * blk, blk)
        acc_ref[...] = jnp.zeros_like(acc_ref)
        run_ref[...] = jnp.zeros_like(run_ref)
        blocks(q0, [q0], True)

        def k_body(n, c):
            k0 = pl.multiple_of((qi - 1 - n) * blk, blk)
            blocks(q0, [k0], False)
            return c

        lax.fori_loop(0, qi, k_body, 0)
        o_ref[pl.ds(q0, blk), :] = acc_ref[...].astype(o_ref.dtype)
        return carry

    lax.fori_loop(0, seq // blk, q_body, 0)


def _attention(p, batch, seq, heads):
    blk = ATTN_BLOCK
    assert seq % blk == 0 and blk % LANES == 0
    width = heads * SB_HEAD_DIM
    return pl.pallas_call(
        functools.partial(_attn_kernel, seq=seq, blk=blk, heads=heads),
        grid=(batch,),
        in_specs=[
            pl.BlockSpec((seq, width), lambda b: (b, 0)),
            pl.BlockSpec((seq, width), lambda b: (b, 1)),
            pl.BlockSpec((seq, width), lambda b: (b, 2)),
        ],
        out_specs=pl.BlockSpec((seq, width), lambda b: (b, 0)),
        out_shape=jax.ShapeDtypeStruct((batch * seq, width), BF16),
        scratch_shapes=[pltpu.VMEM((blk, width), F32), pltpu.VMEM((blk, width), F32)],
        compiler_params=pltpu.CompilerParams(
            dimension_semantics=("parallel",), vmem_limit_bytes=VMEM_LIMIT_BYTES),
        name="stickbreak_attention",
    )(p, p, p)


def _outproj_residual(y_ref, wo_ref, ln_ref, x_ref, o_ref, rows):
    o = jnp.dot(y_ref[rows, :], wo_ref[...], preferred_element_type=F32)
    ms = jnp.mean(o * o, axis=-1, keepdims=True)
    o_ref[rows, :] = x_ref[rows, :] + o * lax.rsqrt(ms + EPS) * ln_ref[...]


def _even_tail_kernel(a_ref, u_ref, uh_ref, g_ref, x_ref, pw_ref, ps_ref, wo_ref, ln_ref, o_ref, y_ref,
                      *, seq, tm, sb_width, group_dim):
    i = pl.program_id(0)
    pos0 = (i * tm) % seq
    halo, chunk = POOL_HALO, POOL_CHUNK
    u_hist0 = jnp.where(pos0 == 0, jnp.zeros((halo, u_ref.shape[1]), BF16), uh_ref[...])

    lag = (lax.broadcasted_iota(jnp.int32, (chunk, chunk), 0)
           - lax.broadcasted_iota(jnp.int32, (chunk, chunk), 1))
    lag_hist = (lax.broadcasted_iota(jnp.int32, (halo, halo), 0) + halo
                - lax.broadcasted_iota(jnp.int32, (halo, halo), 1))

    bands = [jnp.where((lag >= 0) & (lag < win), 1.0, 0.0).astype(BF16) for win in POOL_WINDOWS]
    bands_hist = [jnp.where(lag_hist < win, 1.0, 0.0).astype(BF16) for win in POOL_WINDOWS]

    for r0 in range(0, tm, chunk):
        rows = slice(r0, r0 + chunk)
        y_ref[rows, :sb_width] = (a_ref[rows, :].astype(F32)
                                  * _silu(g_ref[rows, :sb_width].astype(F32))).astype(BF16)
        pos = pos0 + r0 + lax.broadcasted_iota(jnp.int32, (chunk, 1), 0)
        for gi, win in enumerate(POOL_WINDOWS):
            lo_c, hi_c = gi * group_dim, (gi + 1) * group_dim
            ug = u_ref[rows, lo_c:hi_c]
            hist = u_hist0[:, lo_c:hi_c] if r0 == 0 else u_ref[r0 - halo:r0, lo_c:hi_c]
            wsum = jnp.dot(bands[gi], ug, preferred_element_type=F32)
            head = wsum[:halo] + jnp.dot(bands_hist[gi], hist, preferred_element_type=F32)
            wsum = jnp.concatenate([head, wsum[halo:]], axis=0)
            inv_count = 1.0 / jnp.minimum(win, pos + 1).astype(F32)
            pooled = wsum * inv_count - ug.astype(F32)
            yg = jnp.dot(pooled.astype(BF16), pw_ref[gi], preferred_element_type=F32) * ps_ref[:, lo_c:hi_c]
            gate = _silu(g_ref[rows, sb_width + lo_c:sb_width + hi_c].astype(F32))
            y_ref[rows, sb_width + lo_c:sb_width + hi_c] = (yg * gate).astype(BF16)
        _outproj_residual(y_ref, wo_ref, ln_ref, x_ref, o_ref, rows)


def _even_tail(a, p, x2d, pool_w, pool_scale, w_out, ln_post, seq, sb_width):
    m, d = x2d.shape
    tm, halo = EVEN_TM, POOL_HALO
    pool_width = d - sb_width
    groups, group_dim, _ = pool_w.shape
    assert seq % tm == 0 and tm % POOL_CHUNK == 0 and POOL_CHUNK % halo == 0 and groups == len(POOL_WINDOWS)
    assert halo >= max(POOL_WINDOWS) - 1 and pool_width == sb_width and d % pool_width == 0
    u_col = (3 * sb_width) // pool_width
    g_col = (3 * sb_width + pool_width) // d
    assert u_col * pool_width == 3 * sb_width and g_col * d == 3 * sb_width + pool_width
    const = lambda i: (0, 0)
    return pl.pallas_call(
        functools.partial(_even_tail_kernel, seq=seq, tm=tm, sb_width=sb_width, group_dim=group_dim),
        grid=(m // tm,),
        in_specs=[
            pl.BlockSpec((tm, sb_width), lambda i: (i, 0)),
            pl.BlockSpec((tm, pool_width), lambda i: (i, u_col)),
            pl.BlockSpec((halo, pool_width), lambda i: (jnp.maximum(i * (tm // halo) - 1, 0), u_col)),
            pl.BlockSpec((tm, d), lambda i: (i, g_col)),
            pl.BlockSpec((tm, d), lambda i: (i, 0)),
            _resident((groups, group_dim, group_dim), lambda i: (0, 0, 0)),
            _resident((1, pool_width), const),
            _resident((d, d), const),
            _resident((1, d), const),
        ],
        out_specs=pl.BlockSpec((tm, d), lambda i: (i, 0)),
        out_shape=jax.ShapeDtypeStruct((m, d), F32),
        scratch_shapes=[pltpu.VMEM((tm, d), BF16)],
        compiler_params=pltpu.CompilerParams(
            dimension_semantics=("parallel",), vmem_limit_bytes=VMEM_LIMIT_BYTES),
        name="even_tail",
    )(a, p, p, p, x2d, pool_w, pool_scale.reshape(1, pool_width), w_out, ln_post.reshape(1, d))


def _odd_tail_kernel(p_ref, ph_ref, x_ref, sw_ref, dw_ref, db_ref, cg_ref, cb_ref, wo_ref, ln_ref, o_ref,
                     zbuf, dbuf, cbuf, y_ref, *, seq, tm, width):
    i = pl.program_id(0)
    halo = CONV_HALO
    w = width
    planes = w // LANES
    keep = jnp.where((i * tm) % seq == 0, 0.0, 1.0).astype(F32)

    def piece(ref, n, c):
        return ref[:, n * w + c * LANES:n * w + (c + 1) * LANES].astype(F32)

    for c in range(planes):
        zbuf[c, :halo, :] = piece(ph_ref, 2, c) * piece(ph_ref, 0, c) * keep
        zbuf[c, halo:, :] = piece(p_ref, 2, c) * piece(p_ref, 0, c)
        dbuf[c, :halo, :] = piece(ph_ref, 3, c) * _sigmoid(piece(ph_ref, 4, c)) * keep
        dbuf[c, halo:, :] = piece(p_ref, 3, c) * _sigmoid(piece(p_ref, 4, c))

    for ci, r0 in enumerate(range(0, tm, CONV_ROWS)):
        rows = slice(r0, r0 + CONV_ROWS)
        slot = ci % cbuf.shape[0]
        for c in range(planes):
            lanes = slice(c * LANES, (c + 1) * LANES)
            c3 = jnp.zeros((CONV_ROWS, LANES), F32)
            for k in range(SCONV_K):
                off = halo - (SCONV_K - 1) + k + r0
                c3 = c3 + sw_ref[k:k + 1, lanes] * zbuf[c, off:off + CONV_ROWS, :]
            gate_c = _silu(p_ref[rows, 5 * w + c * LANES:5 * w + (c + 1) * LANES].astype(F32))
            bc = p_ref[rows, w + c * LANES:w + (c + 1) * LANES].astype(F32)
            y_ref[rows, lanes] = (bc * c3 * gate_c).astype(BF16)

            acc = jnp.broadcast_to(db_ref[:, lanes], (CONV_ROWS, LANES))
            for k in range(CONF_K):
                off = halo - (CONF_K - 1) + k + r0
                acc = acc + dw_ref[k:k + 1, lanes] * dbuf[c, off:off + CONV_ROWS, :]
            cbuf[slot, :, lanes] = acc

        d = cbuf[slot]
        mu = jnp.mean(d, axis=-1, keepdims=True)
        dc = d - mu
        var = jnp.mean(dc * dc, axis=-1, keepdims=True)
        dn = _silu(dc * lax.rsqrt(var + EPS) * cg_ref[...] + cb_ref[...])
        gate_d = _silu(p_ref[rows, 6 * w:7 * w].astype(F32))
        y_ref[rows, w:] = (dn * gate_d).astype(BF16)
        if (r0 + CONV_ROWS) % OUT_ROWS == 0:
            _outproj_residual(y_ref, wo_ref, ln_ref, x_ref, o_ref, slice(r0 + CONV_ROWS - OUT_ROWS, r0 + CONV_ROWS))


def _odd_tail(p, x2d, sconv_w, dconv_w, dconv_b, cnorm_g, cnorm_b, w_out, ln_post, seq, width):
    m, d = x2d.shape
    n_in = p.shape[1]
    tm, halo = ODD_TM, CONV_HALO
    assert seq % tm == 0 and tm % halo == 0 and tm % OUT_ROWS == 0 and OUT_ROWS % CONV_ROWS == 0
    assert halo >= CONF_K - 1
    assert d == 2 * width and n_in == 7 * width and width % LANES == 0
    assert sconv_w.shape == (SCONV_K, width) and dconv_w.shape == (CONF_K, width)
    const = lambda i: (0, 0)
    return pl.pallas_call(
        functools.partial(_odd_tail_kernel, seq=seq, tm=tm, width=width),
        grid=(m // tm,),
        in_specs=[
            pl.BlockSpec((tm, n_in), lambda i: (i, 0)),
            pl.BlockSpec((halo, n_in), lambda i: (jnp.maximum(i * (tm // halo) - 1, 0), 0)),
            pl.BlockSpec((tm, d), lambda i: (i, 0)),
            _resident((SCONV_K, width), const),
            _resident((CONF_K, width), const),
            _resident((1, width), const),
            _resident((1, width), const),
            _resident((1, width), const),
            _resident((d, d), const),
            _resident((1, d), const),
        ],
        out_specs=pl.BlockSpec((tm, d), lambda i: (i, 0)),
        out_shape=jax.ShapeDtypeStruct((m, d), F32),
        scratch_shapes=[
            pltpu.VMEM((width // LANES, halo + tm, LANES), F32),
            pltpu.VMEM((width // LANES, halo + tm, LANES), F32),
            pltpu.VMEM((2, CONV_ROWS, width), F32),
            pltpu.VMEM((tm, d), BF16),
        ],
        compiler_params=pltpu.CompilerParams(
            dimension_semantics=("parallel",), vmem_limit_bytes=VMEM_LIMIT_BYTES),
        name="odd_tail",
    )(p, p, x2d, sconv_w, dconv_w, dconv_b.reshape(1, width), cnorm_g.reshape(1, width),
      cnorm_b.reshape(1, width), w_out, ln_post.reshape(1, d))


def kernel(x, ln_pre_even, w_in_even, pool_w, pool_scale, w_out_even, ln_post_even,
           ln_pre_odd, w_in_odd, sconv_w, dconv_w, dconv_b, cnorm_g, cnorm_b, w_out_odd, ln_post_odd):
    batch, seq, d = x.shape
    depth = ln_pre_even.shape[0] + ln_pre_odd.shape[0]
    sb_width = d // 2
    heads = sb_width // SB_HEAD_DIM
    conv_width = d // 2
    q_scale = math.log2(math.e) / math.sqrt(SB_HEAD_DIM)

    xf = x.reshape(batch * seq, d)
    for layer in range(depth):
        i = layer // 2
        if layer % 2 == 0:
            n_in = w_in_even.shape[-1]
            col_scale = jnp.where(jnp.arange(n_in) < sb_width, q_scale, 1.0).astype(F32)
            p = _norm_inproj(xf, ln_pre_even[i], w_in_even[i].astype(BF16), col_scale)
            a = _attention(p, batch, seq, heads)
            xf = _even_tail(a, p, xf, pool_w[i].astype(BF16), pool_scale[i], w_out_even[i].astype(BF16),
                            ln_post_even[i], seq, sb_width)
        else:
            p = _norm_inproj(xf, ln_pre_odd[i], w_in_odd[i].astype(BF16))
            xf = _odd_tail(p, xf, sconv_w[i], dconv_w[i], dconv_b[i], cnorm_g[i], cnorm_b[i],
                           w_out_odd[i].astype(BF16), ln_post_odd[i], seq, conv_width)
    return xf.reshape(batch, seq, d)
```

```python
import functools
import math

import jax
import jax.numpy as jnp
from jax import lax
from jax.experimental import pallas as pl
from jax.experimental.pallas import tpu as pltpu

F32 = jnp.float32
BF16 = jnp.bfloat16
EPS = 1e-6

SB_HEAD_DIM = 128
POOL_WINDOWS = (2, 4, 8, 16)
SCONV_K = 3
CONF_K = 31

VMEM_LIMIT_BYTES = 56 * 1024 * 1024
LANES = 128
MXU_DIM = 256

FRONT_TM = 512
BACK_TM = 512
ATTN_BLOCK = 256
POOL_CHUNK = 256
POOL_HALO = 16
CONV_HALO = 32
CONV_ROWS = 128


def _sigmoid(v):
    return 1.0 / (1.0 + jnp.exp(-v))


def _silu(v):
    return v * _sigmoid(v)


def _resident(block_shape, index_map):
    return pl.BlockSpec(block_shape, index_map, pipeline_mode=pl.Buffered(1))


def _rms_norm_to(h_ref, x_ref, g_ref):
    x = x_ref[...]
    ms = jnp.mean(x * x, axis=-1, keepdims=True)
    h_ref[...] = (x * lax.rsqrt(ms + EPS) * g_ref[...]).astype(h_ref.dtype)


def _emit_interleaved(main, side):
    done = 0
    for t, thunk in enumerate(main):
        thunk()
        upto = (t + 1) * len(side) // len(main)
        for s in side[done:upto]:
            s()
        done = upto


def _proj_chunks(h_ref, w_ref, store, row_chunk=None):
    tm = h_ref.shape[0]
    row_chunk = row_chunk or tm

    def piece(rows, cols):
        return lambda: store(rows, cols, jnp.dot(h_ref[rows, :], w_ref[:, cols], preferred_element_type=F32))

    return [piece(slice(r0, r0 + row_chunk), slice(c0, c0 + MXU_DIM))
            for c0 in range(0, w_ref.shape[1], MXU_DIM) for r0 in range(0, tm, row_chunk)]


def _front_even_kernel(x_ref, g_ref, w_ref, cs_ref, pw_ref, ps_ref, qk_ref, v_ref, gp_ref,
                       h_ref, u_st, g_st, yp_st, *, seq, tm, width, group_dim):
    i, j = pl.program_id(0), pl.program_id(1)
    halo, chunk = POOL_HALO, POOL_CHUNK
    pos0 = (i * tm) % seq

    @pl.when((i == 0) & (j == 0))
    def _():
        u_st[tm:, :] = jnp.zeros((halo, width), BF16)

    @pl.when(j == 0)
    def _():
        _rms_norm_to(h_ref, x_ref, g_ref)
        vu = jnp.dot(h_ref[...], w_ref[...], preferred_element_type=F32)
        v_ref[...] = vu[:, :width].astype(BF16)
        u_st[:halo, :] = jnp.where(pos0 == 0, jnp.zeros((halo, width), BF16), u_st[tm:, :])
        u_st[halo:, :] = vu[:, width:].astype(BF16)

    @pl.when(j == 1)
    def _():
        def store_g(rows, cols, val):
            g_st[rows, cols] = val.astype(BF16)

        lag = (lax.broadcasted_iota(jnp.int32, (chunk, chunk), 0)
               - lax.broadcasted_iota(jnp.int32, (chunk, chunk), 1))
        lag_hist = (lax.broadcasted_iota(jnp.int32, (halo, halo), 0) + halo
                    - lax.broadcasted_iota(jnp.int32, (halo, halo), 1))

        def pool_unit(gi, win, r0):
            cols = slice(gi * group_dim, (gi + 1) * group_dim)
            band = jnp.where((lag >= 0) & (lag < win), 1.0, 0.0).astype(BF16)
            band_hist = jnp.where(lag_hist < win, 1.0, 0.0).astype(BF16)
            ug = u_st[halo + r0:halo + r0 + chunk, cols]
            hist = u_st[r0:r0 + halo, cols]
            wsum = jnp.dot(band, ug, preferred_element_type=F32)
            head = wsum[:halo] + jnp.dot(band_hist, hist, preferred_element_type=F32)
            wsum = jnp.concatenate([head, wsum[halo:]], axis=0)
            pos = pos0 + r0 + lax.broadcasted_iota(jnp.int32, (chunk, 1), 0)
            inv_count = 1.0 / jnp.minimum(win, pos + 1).astype(F32)
            pooled = wsum * inv_count - ug.astype(F32)
            yp_st[r0:r0 + chunk, cols] = (jnp.dot(pooled.astype(BF16), pw_ref[gi], preferred_element_type=F32)
                                          * ps_ref[:, cols])

        side = [functools.partial(pool_unit, gi, win, r0)
                for gi, win in enumerate(POOL_WINDOWS) for r0 in range(0, tm, chunk)]
        _emit_interleaved(_proj_chunks(h_ref, w_ref, store_g), side)

    @pl.when(j == 2)
    def _():
        def store_qk(rows, cols, val):
            qk_ref[rows, cols] = (val * cs_ref[:, cols]).astype(BF16)

        def gate_a(rows):
            gp_ref[rows, :width] = _silu(g_st[rows, :width].astype(F32)).astype(BF16)

        def gate_p(rows):
            gp_ref[rows, width:] = (yp_st[rows, :] * _silu(g_st[rows, width:].astype(F32))).astype(BF16)

        side = [functools.partial(unit, slice(r0, r0 + LANES))
                for r0 in range(0, tm, LANES) for unit in (gate_a, gate_p)]
        _emit_interleaved(_proj_chunks(h_ref, w_ref, store_qk), side)


def _front_even(x2d, gain, w_in, q_scale, pool_w, pool_scale, seq, width):
    m, d = x2d.shape
    tm, halo = FRONT_TM, POOL_HALO
    groups, group_dim, _ = pool_w.shape
    assert w_in.shape == (d, 6 * width) and d == 2 * width and groups * group_dim == width
    assert seq % tm == 0 and tm % POOL_CHUNK == 0 and halo >= max(POOL_WINDOWS) - 1 and groups == len(POOL_WINDOWS)
    col_scale = jnp.where(jnp.arange(2 * width) < width, q_scale, 1.0).astype(F32).reshape(1, 2 * width)
    const2 = lambda i, j: (0, 0)
    row_block = lambda i, j: (i, 0)
    bf16_out = lambda n: jax.ShapeDtypeStruct((m, n), BF16)
    return pl.pallas_call(
        functools.partial(_front_even_kernel, seq=seq, tm=tm, width=width, group_dim=group_dim),
        grid=(m // tm, 3),
        in_specs=[
            pl.BlockSpec((tm, d), row_block),
            _resident((1, d), const2),
            pl.BlockSpec((d, 2 * width), lambda i, j: (0, (j + 1) % 3)),
            _resident((1, 2 * width), const2),
            _resident((groups, group_dim, group_dim), lambda i, j: (0, 0, 0)),
            _resident((1, width), const2),
        ],
        out_specs=[
            pl.BlockSpec((tm, 2 * width), row_block),
            pl.BlockSpec((tm, width), row_block),
            pl.BlockSpec((tm, 2 * width), row_block),
        ],
        out_shape=[bf16_out(2 * width), bf16_out(width), bf16_out(2 * width)],
        scratch_shapes=[
            pltpu.VMEM((tm, d), BF16),
            pltpu.VMEM((halo + tm, width), BF16),
            pltpu.VMEM((tm, 2 * width), BF16),
            pltpu.VMEM((tm, width), F32),
        ],
        compiler_params=pltpu.CompilerParams(
            dimension_semantics=("arbitrary", "arbitrary"), vmem_limit_bytes=VMEM_LIMIT_BYTES),
        name="front_even",
    )(x2d, gain.reshape(1, d), w_in, col_scale, pool_w, pool_scale.reshape(1, width))


def _attn_kernel(q_ref, k_ref, v_ref, o_ref, acc_ref, run_ref, *, seq, blk, heads):
    dh = SB_HEAD_DIM
    row = lax.broadcasted_iota(jnp.int32, (blk, blk), 0)
    col = lax.broadcasted_iota(jnp.int32, (blk, blk), 1)
    later_keys = jnp.where(row > col, 1.0, 0.0).astype(BF16)
    causal = col < row

    def block(q0, k0, masked):
        log_beta, stay_terms = [], []
        for h in range(heads):
            lanes = slice(h * dh, (h + 1) * dh)
            q = q_ref[pl.ds(q0, blk), lanes]
            kb = k_ref[pl.ds(k0, blk), lanes]
            z = lax.dot_general(q, kb, (((1,), (1,)), ((), ())), preferred_element_type=F32)
            pos = jnp.maximum(z, 0.0)
            neg = jnp.minimum(z, 0.0)
            tail = jnp.log2(1.0 + jnp.exp2(neg - pos))
            log_beta.append(neg - tail)
            sp = pos + tail
            stay_terms.append(jnp.where(causal, sp, 0.0) if masked else sp)
        stacked = jnp.concatenate([s.astype(BF16) for s in stay_terms], axis=0)
        later = jnp.dot(stacked, later_keys, preferred_element_type=F32)
        for h in range(heads):
            lanes = slice(h * dh, (h + 1) * dh)
            run = run_ref[:, lanes]
            arg = log_beta[h] - later[h * blk:(h + 1) * blk] - jnp.concatenate([run] * (blk // LANES), axis=1)
            w = jnp.exp2(arg)
            if masked:
                w = jnp.where(causal, w, 0.0)
            vb = v_ref[pl.ds(k0, blk), lanes]
            acc_ref[:, lanes] += jnp.dot(w.astype(BF16), vb, preferred_element_type=F32)
            run_ref[:, lanes] = run + jnp.sum(stay_terms[h], axis=-1, keepdims=True)

    def q_body(qi, carry):
        q0 = pl.multiple_of(qi * blk, blk)
        acc_ref[...] = jnp.zeros_like(acc_ref)
        run_ref[...] = jnp.zeros_like(run_ref)
        block(q0, q0, True)

        def k_body(n, c):
            k0 = pl.multiple_of((qi - 1 - n) * blk, blk)
            block(q0, k0, False)
            return c

        lax.fori_loop(0, qi, k_body, 0)
        o_ref[pl.ds(q0, blk), :] = acc_ref[...].astype(o_ref.dtype)
        return carry

    lax.fori_loop(0, seq // blk, q_body, 0)


def _attention(qk, v, batch, seq, heads):
    blk = ATTN_BLOCK
    assert seq % blk == 0 and blk % LANES == 0
    width = heads * SB_HEAD_DIM
    return pl.pallas_call(
        functools.partial(_attn_kernel, seq=seq, blk=blk, heads=heads),
        grid=(batch,),
        in_specs=[
            pl.BlockSpec((seq, width), lambda b: (b, 0)),
            pl.BlockSpec((seq, width), lambda b: (b, 1)),
            pl.BlockSpec((seq, width), lambda b: (b, 0)),
        ],
        out_specs=pl.BlockSpec((seq, width), lambda b: (b, 0)),
        out_shape=jax.ShapeDtypeStruct((batch * seq, width), BF16),
        scratch_shapes=[pltpu.VMEM((blk, width), F32), pltpu.VMEM((blk, width), F32)],
        compiler_params=pltpu.CompilerParams(
            dimension_semantics=("parallel",), vmem_limit_bytes=VMEM_LIMIT_BYTES),
        name="stickbreak_attention",
    )(qk, qk, v)


def _post_norm_residual(o, ln_ref, x_ref, o_ref):
    ms = jnp.mean(o * o, axis=-1, keepdims=True)
    o_ref[...] = x_ref[...] + o * lax.rsqrt(ms + EPS) * ln_ref[...]


def _back_even_kernel(a_ref, gp_ref, x_ref, wo_ref, ln_ref, o_ref, *, width):
    ya = (a_ref[...].astype(F32) * gp_ref[:, :width].astype(F32)).astype(BF16)
    o = (jnp.dot(ya, wo_ref[:width, :], preferred_element_type=F32)
         + jnp.dot(gp_ref[:, width:], wo_ref[width:, :], preferred_element_type=F32))
    _post_norm_residual(o, ln_ref, x_ref, o_ref)


def _back_odd_kernel(y_ref, x_ref, wo_ref, ln_ref, o_ref):
    o = jnp.dot(y_ref[...], wo_ref[...], preferred_element_type=F32)
    _post_norm_residual(o, ln_ref, x_ref, o_ref)


def _back(body, name, mixed, x2d, w_out, ln_post):
    m, d = x2d.shape
    tm = BACK_TM
    assert m % tm == 0
    const = lambda i: (0, 0)
    row_block = lambda i: (i, 0)
    return pl.pallas_call(
        body,
        grid=(m // tm,),
        in_specs=([pl.BlockSpec((tm, a.shape[1]), row_block) for a in mixed]
                  + [pl.BlockSpec((tm, d), row_block), _resident((d, d), const), _resident((1, d), const)]),
        out_specs=pl.BlockSpec((tm, d), row_block),
        out_shape=jax.ShapeDtypeStruct((m, d), F32),
        compiler_params=pltpu.CompilerParams(
            dimension_semantics=("parallel",), vmem_limit_bytes=VMEM_LIMIT_BYTES),
        name=name,
    )(*mixed, x2d, w_out, ln_post.reshape(1, d))


_G_C, _G_D, _GA, _GB, _HC, _CC = range(6)


def _front_odd_kernel(x_ref, g_ref, wa_ref, wb_ref, sw_ref, dw_ref, db_ref, cg_ref, cb_ref, y_ref,
                      h_ref, st, dbuf, zbuf, cbuf, c3buf, *, seq, tm, width):
    i, j = pl.program_id(0), pl.program_id(1)
    halo = CONV_HALO
    w = width
    planes = w // LANES
    seq_start = (i * tm) % seq == 0

    def proj(w_ref):
        return jnp.dot(h_ref[...], w_ref[...], preferred_element_type=F32)

    def carry_history(buf, c):
        buf[c, :halo, :] = jnp.where(seq_start, jnp.zeros((halo, LANES), F32), buf[c, tm:, :])

    @pl.when((i == 0) & (j == 0))
    def _():
        for c in range(planes):
            dbuf[c, tm:, :] = jnp.zeros((halo, LANES), F32)
            zbuf[c, tm:, :] = jnp.zeros((halo, LANES), F32)

    @pl.when(j == 0)
    def _():
        _rms_norm_to(h_ref, x_ref, g_ref)
        st[_G_C] = proj(wa_ref).astype(BF16)
        st[_G_D] = proj(wb_ref).astype(BF16)

    def stash(slot):
        def store(rows, cols, val):
            st[slot, rows, cols] = val.astype(BF16)
        return store

    def plane_lanes(c):
        return slice(c * LANES, (c + 1) * LANES)

    row_chunks = [slice(r0, r0 + CONV_ROWS) for r0 in range(0, tm, CONV_ROWS)]

    @pl.when(j == 1)
    def _():
        def gate_unit(slot, rows):
            st[slot, rows, :] = _silu(st[slot, rows, :].astype(F32)).astype(BF16)

        side = [functools.partial(gate_unit, slot, rows) for slot in (_G_C, _G_D) for rows in row_chunks]
        _emit_interleaved(_proj_chunks(h_ref, wa_ref, stash(_GA)) + _proj_chunks(h_ref, wb_ref, stash(_GB)), side)

    @pl.when(j == 2)
    def _():
        def fill_unit(c):
            lanes = plane_lanes(c)
            carry_history(dbuf, c)
            dbuf[c, halo:, :] = st[_GA, :, lanes].astype(F32) * _sigmoid(st[_GB, :, lanes].astype(F32))

        def conv_unit(c, rows):
            lanes = plane_lanes(c)
            acc = jnp.broadcast_to(db_ref[:, lanes], (CONV_ROWS, LANES))
            for k in range(CONF_K):
                off = halo - (CONF_K - 1) + k + rows.start
                acc = acc + dw_ref[k:k + 1, lanes] * dbuf[c, off:off + CONV_ROWS, :]
            cbuf[rows, lanes] = acc

        side = []
        for c in range(planes):
            side.append(functools.partial(fill_unit, c))
            if c > 0:
                side.extend(functools.partial(conv_unit, c - 1, rows) for rows in row_chunks)
        side.extend(functools.partial(conv_unit, planes - 1, rows) for rows in row_chunks)
        main = (_proj_chunks(h_ref, wa_ref, stash(_HC), CONV_ROWS)
                + _proj_chunks(h_ref, wb_ref, stash(_CC), CONV_ROWS))
        _emit_interleaved(main, side)

    @pl.when(j == 3)
    def _():
        def norm_unit(rows):
            d = cbuf[rows, :]
            mu = jnp.mean(d, axis=-1, keepdims=True)
            dc = d - mu
            var = jnp.mean(dc * dc, axis=-1, keepdims=True)
            dn = _silu(dc * lax.rsqrt(var + EPS) * cg_ref[...] + cb_ref[...])
            y_ref[rows, w:] = (dn * st[_G_D, rows, :].astype(F32)).astype(BF16)

        def fill_unit(c):
            lanes = plane_lanes(c)
            carry_history(zbuf, c)
            zbuf[c, halo:, :] = st[_CC, :, lanes].astype(F32) * st[_HC, :, lanes].astype(F32)

        def conv_unit(c, rows):
            lanes = plane_lanes(c)
            c3 = jnp.zeros((CONV_ROWS, LANES), F32)
            for k in range(SCONV_K):
                off = halo - (SCONV_K - 1) + k + rows.start
                c3 = c3 + sw_ref[k:k + 1, lanes] * zbuf[c, off:off + CONV_ROWS, :]
            c3buf[rows, lanes] = c3 * st[_G_C, rows, lanes].astype(F32)

        side = [functools.partial(norm_unit, rows) for rows in row_chunks]
        for c in range(planes):
            side.append(functools.partial(fill_unit, c))
            side.extend(functools.partial(conv_unit, c, rows) for rows in row_chunks)
        _emit_interleaved(_proj_chunks(h_ref, wa_ref, stash(_GA)), side)
        y_ref[:, :w] = (st[_GA].astype(F32) * c3buf[...]).astype(BF16)


def _front_odd(x2d, gain, w_in, sconv_w, dconv_w, dconv_b, cnorm_g, cnorm_b, seq, width):
    m, d = x2d.shape
    tm, halo = FRONT_TM, CONV_HALO
    w = width
    assert w_in.shape == (d, 7 * w) and d == 2 * w and w % LANES == 0
    assert seq % tm == 0 and tm % CONV_ROWS == 0 and halo >= CONF_K - 1 and halo % 8 == 0
    assert sconv_w.shape == (SCONV_K, w) and dconv_w.shape == (CONF_K, w)
    const2 = lambda i, j: (0, 0)
    row_block = lambda i, j: (i, 0)
    col_a = lambda i, j: (0, jnp.where(j == 0, 5, jnp.where(j == 1, 3, jnp.where(j == 2, 0, 1))))
    col_b = lambda i, j: (0, jnp.where(j == 0, 6, jnp.where(j == 1, 4, 2)))
    planes = w // LANES
    return pl.pallas_call(
        functools.partial(_front_odd_kernel, seq=seq, tm=tm, width=w),
        grid=(m // tm, 4),
        in_specs=[
            pl.BlockSpec((tm, d), row_block),
            _resident((1, d), const2),
            pl.BlockSpec((d, w), col_a),
            pl.BlockSpec((d, w), col_b),
            _resident((SCONV_K, w), const2),
            _resident((CONF_K, w), const2),
            _resident((1, w), const2),
            _resident((1, w), const2),
            _resident((1, w), const2),
        ],
        out_specs=pl.BlockSpec((tm, d), row_block),
        out_shape=jax.ShapeDtypeStruct((m, d), BF16),
        scratch_shapes=[
            pltpu.VMEM((tm, d), BF16),
            pltpu.VMEM((6, tm, w), BF16),
            pltpu.VMEM((planes, halo + tm, LANES), F32),
            pltpu.VMEM((planes, halo + tm, LANES), F32),
            pltpu.VMEM((tm, w), F32),
            pltpu.VMEM((tm, w), F32),
        ],
        compiler_params=pltpu.CompilerParams(
            dimension_semantics=("arbitrary", "arbitrary"), vmem_limit_bytes=VMEM_LIMIT_BYTES,
        ),
        name="front_odd",
    )(x2d, gain.reshape(1, d), w_in, w_in, sconv_w, dconv_w, dconv_b.reshape(1, w),
      cnorm_g.reshape(1, w), cnorm_b.reshape(1, w))


def kernel(x, ln_pre_even, w_in_even, pool_w, pool_scale, w_out_even, ln_post_even,
           ln_pre_odd, w_in_odd, sconv_w, dconv_w, dconv_b, cnorm_g, cnorm_b, w_out_odd, ln_post_odd):
    batch, seq, d = x.shape
    depth = ln_pre_even.shape[0] + ln_pre_odd.shape[0]
    width = d // 2
    heads = width // SB_HEAD_DIM
    q_scale = math.log2(math.e) / math.sqrt(SB_HEAD_DIM)

    xf = x.reshape(batch * seq, d)
    for layer in range(depth):
        i = layer // 2
        if layer % 2 == 0:
            qk, v, gp = _front_even(xf, ln_pre_even[i], w_in_even[i].astype(BF16), q_scale,
                                    pool_w[i].astype(BF16), pool_scale[i], seq, width)
            a = _attention(qk, v, batch, seq, heads)
            xf = _back(functools.partial(_back_even_kernel, width=width), "back_even", [a, gp], xf,
                       w_out_even[i].astype(BF16), ln_post_even[i])
        else:
            y = _front_odd(xf, ln_pre_odd[i], w_in_odd[i].astype(BF16), sconv_w[i], dconv_w[i], dconv_b[i],
                           cnorm_g[i], cnorm_b[i], seq, width)
            xf = _back(_back_odd_kernel, "back_odd", [y], xf, w_out_odd[i].astype(BF16), ln_post_odd[i])
    return xf.reshape(batch, seq, d)
```

```python
import functools
import math

import jax
import jax.numpy as jnp
from jax import lax
from jax.experimental import pallas as pl
from jax.experimental.pallas import tpu as pltpu

F32 = jnp.float32
BF16 = jnp.bfloat16
EPS = 1e-6

SB_HEAD_DIM = 128
POOL_WINDOWS = (2, 4, 8, 16)
SCONV_K = 3
CONF_K = 31

VMEM_LIMIT_BYTES = 56 * 1024 * 1024
LANES = 128
MXU_DIM = 256

INPROJ_TM = 1024
INPROJ_TN_MAX = 2048
ATTN_BLOCK = 256
HEAD_GROUPS = 2
EVEN_TM = 512
ODD_TM = 512
POOL_CHUNK = 256
POOL_HALO = 16
CONV_HALO = 32
CONV_ROWS = 128
OUT_ROWS = 256


def _sigmoid(v):
    return 1.0 / (1.0 + jnp.exp(-v))


def _silu(v):
    return v * _sigmoid(v)


def _resident(block_shape, index_map):
    return pl.BlockSpec(block_shape, index_map, pipeline_mode=pl.Buffered(1))


def _norm_inproj_kernel(x_ref, g_ref, w_ref, *rest):
    *maybe_scale_ref, o_ref, h_ref = rest

    @pl.when(pl.program_id(1) == 0)
    def _():
        x = x_ref[...]
        ms = jnp.mean(x * x, axis=-1, keepdims=True)
        h_ref[...] = (x * lax.rsqrt(ms + EPS) * g_ref[...]).astype(BF16)

    o = jnp.dot(h_ref[...], w_ref[...], preferred_element_type=F32)
    if maybe_scale_ref:
        o = o * maybe_scale_ref[0][...]
    o_ref[...] = o.astype(o_ref.dtype)


def _norm_inproj(x2d, gain, w, col_scale=None):
    m, d = x2d.shape
    n = w.shape[1]
    tm = INPROJ_TM
    tn = max(t for t in range(MXU_DIM, INPROJ_TN_MAX + 1, MXU_DIM) if n % t == 0)
    assert m % tm == 0
    in_specs = [
        pl.BlockSpec((tm, d), lambda i, j: (i, 0)),
        _resident((1, d), lambda i, j: (0, 0)),
        pl.BlockSpec((d, tn), lambda i, j: (0, j)),
    ]
    operands = [x2d, gain.reshape(1, d), w]
    if col_scale is not None:
        in_specs.append(pl.BlockSpec((1, tn), lambda i, j: (0, j)))
        operands.append(col_scale.reshape(1, n))
    return pl.pallas_call(
        _norm_inproj_kernel,
        grid=(m // tm, n // tn),
        in_specs=in_specs,
        out_specs=pl.BlockSpec((tm, tn), lambda i, j: (i, j)),
        out_shape=jax.ShapeDtypeStruct((m, n), BF16),
        scratch_shapes=[pltpu.VMEM((tm, d), BF16)],
        compiler_params=pltpu.CompilerParams(
            dimension_semantics=("parallel", "arbitrary"), vmem_limit_bytes=VMEM_LIMIT_BYTES),
        name="norm_inproj",
    )(*operands)


def _attn_kernel(q_ref, k_ref, v_ref, o_ref, acc_ref, run_ref, *, seq, blk, heads):
    dh = SB_HEAD_DIM
    row = lax.broadcasted_iota(jnp.int32, (blk, blk), 0)
    col = lax.broadcasted_iota(jnp.int32, (blk, blk), 1)
    later_keys = jnp.where(row > col, 1.0, 0.0).astype(BF16)
    causal = col < row

    def block(q0, k0, masked):
        group = heads // HEAD_GROUPS
        log_beta, stay_terms, later = {}, {}, {}
        for g in range(HEAD_GROUPS):
            group_heads = range(g * group, (g + 1) * group)
            for h in group_heads:
                lanes = slice(h * dh, (h + 1) * dh)
                q = q_ref[pl.ds(q0, blk), lanes]
                kb = k_ref[pl.ds(k0, blk), lanes]
                z = lax.dot_general(q, kb, (((1,), (1,)), ((), ())), preferred_element_type=F32)
                tail = jnp.log2(1.0 + jnp.exp2(-jnp.abs(z)))
                lb = jnp.minimum(z, 0.0) - tail
                sp = z - lb
                log_beta[h] = lb
                stay_terms[h] = jnp.where(causal, sp, 0.0) if masked else sp
            stacked = jnp.concatenate([stay_terms[h].astype(BF16) for h in group_heads], axis=0)
            sums = jnp.dot(stacked, later_keys, preferred_element_type=F32)
            for n, h in enumerate(group_heads):
                later[h] = sums[n * blk:(n + 1) * blk]
        for h in range(heads):
            lanes = slice(h * dh, (h + 1) * dh)
            run = run_ref[:, lanes]
            w = jnp.exp2(log_beta[h] - later[h])
            if masked:
                w = jnp.where(causal, w, 0.0)
            vb = v_ref[pl.ds(k0, blk), lanes]
            pv = jnp.dot(w.astype(BF16), vb, preferred_element_type=F32)
            acc_ref[:, lanes] += jnp.exp2(-run) * pv
            run_ref[:, lanes] = run + jnp.sum(stay_terms[h], axis=-1, keepdims=True)

    def q_body(qi, carry):
        q0 = pl.multiple_of(qi * blk, blk)
        acc_ref[...] = jnp.zeros_like(acc_ref)
        run_ref[...] = jnp.zeros_like(run_ref)
        block(q0, q0, True)

        def k_body(n, c):
            k0 = pl.multiple_of((qi - 1 - n) * blk, blk)
            block(q0, k0, False)
            return c

        lax.fori_loop(0, qi, k_body, 0)
        o_ref[pl.ds(q0, blk), :] = acc_ref[...].astype(o_ref.dtype)
        return carry

    lax.fori_loop(0, seq // blk, q_body, 0)


def _attention(p, batch, seq, heads):
    blk = ATTN_BLOCK
    assert seq % blk == 0 and SB_HEAD_DIM == LANES
    width = heads * SB_HEAD_DIM
    return pl.pallas_call(
        functools.partial(_attn_kernel, seq=seq, blk=blk, heads=heads),
        grid=(batch,),
        in_specs=[
            pl.BlockSpec((seq, width), lambda b: (b, 0)),
            pl.BlockSpec((seq, width), lambda b: (b, 1)),
            pl.BlockSpec((seq, width), lambda b: (b, 2)),
        ],
        out_specs=pl.BlockSpec((seq, width), lambda b: (b, 0)),
        out_shape=jax.ShapeDtypeStruct((batch * seq, width), BF16),
        scratch_shapes=[pltpu.VMEM((blk, width), F32), pltpu.VMEM((blk, width), F32)],
        compiler_params=pltpu.CompilerParams(
            dimension_semantics=("parallel",), vmem_limit_bytes=VMEM_LIMIT_BYTES),
        name="stickbreak_attention",
    )(p, p, p)


def _outproj_residual(y_ref, wo_ref, ln_ref, x_ref, o_ref, rows):
    o = jnp.dot(y_ref[rows, :], wo_ref[...], preferred_element_type=F32)
    ms = jnp.mean(o * o, axis=-1, keepdims=True)
    o_ref[rows, :] = x_ref[rows, :] + o * lax.rsqrt(ms + EPS) * ln_ref[...]


def _even_tail_kernel(a_ref, u_ref, uh_ref, g_ref, x_ref, pw_ref, ps_ref, wo_ref, ln_ref, o_ref, y_ref,
                      *, seq, tm, sb_width, group_dim):
    i = pl.program_id(0)
    pos0 = (i * tm) % seq
    halo, chunk = POOL_HALO, POOL_CHUNK
    u_hist0 = jnp.where(pos0 == 0, jnp.zeros((halo, u_ref.shape[1]), BF16), uh_ref[...])

    lag = (lax.broadcasted_iota(jnp.int32, (chunk, chunk), 0)
           - lax.broadcasted_iota(jnp.int32, (chunk, chunk), 1))
    lag_hist = (lax.broadcasted_iota(jnp.int32, (halo, halo), 0) + halo
                - lax.broadcasted_iota(jnp.int32, (halo, halo), 1))

    bands = [jnp.where((lag >= 0) & (lag < win), 1.0, 0.0).astype(BF16) for win in POOL_WINDOWS]
    bands_hist = [jnp.where(lag_hist < win, 1.0, 0.0).astype(BF16) for win in POOL_WINDOWS]

    for r0 in range(0, tm, chunk):
        rows = slice(r0, r0 + chunk)
        y_ref[rows, :sb_width] = (a_ref[rows, :].astype(F32)
                                  * _silu(g_ref[rows, :sb_width].astype(F32))).astype(BF16)
        pos = pos0 + r0 + lax.broadcasted_iota(jnp.int32, (chunk, 1), 0)
        for gi, win in enumerate(POOL_WINDOWS):
            lo_c, hi_c = gi * group_dim, (gi + 1) * group_dim
            ug = u_ref[rows, lo_c:hi_c]
            hist = u_hist0[:, lo_c:hi_c] if r0 == 0 else u_ref[r0 - halo:r0, lo_c:hi_c]
            wsum = jnp.dot(bands[gi], ug, preferred_element_type=F32)
            head = wsum[:halo] + jnp.dot(bands_hist[gi], hist, preferred_element_type=F32)
            wsum = jnp.concatenate([head, wsum[halo:]], axis=0)
            inv_count = 1.0 / jnp.minimum(win, pos + 1).astype(F32)
            pooled = wsum * inv_count - ug.astype(F32)
            yg = jnp.dot(pooled.astype(BF16), pw_ref[gi], preferred_element_type=F32) * ps_ref[:, lo_c:hi_c]
            gate = _silu(g_ref[rows, sb_width + lo_c:sb_width + hi_c].astype(F32))
            y_ref[rows, sb_width + lo_c:sb_width + hi_c] = (yg * gate).astype(BF16)
        _outproj_residual(y_ref, wo_ref, ln_ref, x_ref, o_ref, rows)


def _even_tail(a, p, x2d, pool_w, pool_scale, w_out, ln_post, seq, sb_width):
    m, d = x2d.shape
    tm, halo = EVEN_TM, POOL_HALO
    pool_width = d - sb_width
    groups, group_dim, _ = pool_w.shape
    assert seq % tm == 0 and tm % POOL_CHUNK == 0 and POOL_CHUNK % halo == 0 and groups == len(POOL_WINDOWS)
    assert halo >= max(POOL_WINDOWS) - 1 and pool_width == sb_width and d % pool_width == 0
    u_col = (3 * sb_width) // pool_width
    g_col = (3 * sb_width + pool_width) // d
    assert u_col * pool_width == 3 * sb_width and g_col * d == 3 * sb_width + pool_width
    const = lambda i: (0, 0)
    return pl.pallas_call(
        functools.partial(_even_tail_kernel, seq=seq, tm=tm, sb_width=sb_width, group_dim=group_dim),
        grid=(m // tm,),
        in_specs=[
            pl.BlockSpec((tm, sb_width), lambda i: (i, 0)),
            pl.BlockSpec((tm, pool_width), lambda i: (i, u_col)),
            pl.BlockSpec((halo, pool_width), lambda i: (jnp.maximum(i * (tm // halo) - 1, 0), u_col)),
            pl.BlockSpec((tm, d), lambda i: (i, g_col)),
            pl.BlockSpec((tm, d), lambda i: (i, 0)),
            _resident((groups, group_dim, group_dim), lambda i: (0, 0, 0)),
            _resident((1, pool_width), const),
            _resident((d, d), const),
            _resident((1, d), const),
        ],
        out_specs=pl.BlockSpec((tm, d), lambda i: (i, 0)),
        out_shape=jax.ShapeDtypeStruct((m, d), F32),
        scratch_shapes=[pltpu.VMEM((tm, d), BF16)],
        compiler_params=pltpu.CompilerParams(
            dimension_semantics=("parallel",), vmem_limit_bytes=VMEM_LIMIT_BYTES),
        name="even_tail",
    )(a, p, p, p, x2d, pool_w, pool_scale.reshape(1, pool_width), w_out, ln_post.reshape(1, d))


def _odd_tail_kernel(p_ref, ph_ref, x_ref, sw_ref, dw_ref, db_ref, cg_ref, cb_ref, wo_ref, ln_ref, o_ref,
                     zbuf, dbuf, cbuf, y_ref, *, seq, tm, width):
    i = pl.program_id(0)
    halo = CONV_HALO
    w = width
    planes = w // LANES
    keep = jnp.where((i * tm) % seq == 0, 0.0, 1.0).astype(F32)

    def piece(ref, n, c):
        return ref[:, n * w + c * LANES:n * w + (c + 1) * LANES].astype(F32)

    for c in range(planes):
        zbuf[c, :halo, :] = piece(ph_ref, 2, c) * piece(ph_ref, 0, c) * keep
        zbuf[c, halo:, :] = piece(p_ref, 2, c) * piece(p_ref, 0, c)
        dbuf[c, :halo, :] = piece(ph_ref, 3, c) * _sigmoid(piece(ph_ref, 4, c)) * keep
        dbuf[c, halo:, :] = piece(p_ref, 3, c) * _sigmoid(piece(p_ref, 4, c))

    for ci, r0 in enumerate(range(0, tm, CONV_ROWS)):
        rows = slice(r0, r0 + CONV_ROWS)
        slot = ci % cbuf.shape[0]
        for c in range(planes):
            lanes = slice(c * LANES, (c + 1) * LANES)
            c3 = jnp.zeros((CONV_ROWS, LANES), F32)
            for k in range(SCONV_K):
                off = halo - (SCONV_K - 1) + k + r0
                c3 = c3 + sw_ref[k:k + 1, lanes] * zbuf[c, off:off + CONV_ROWS, :]
            gate_c = _silu(p_ref[rows, 5 * w + c * LANES:5 * w + (c + 1) * LANES].astype(F32))
            bc = p_ref[rows, w + c * LANES:w + (c + 1) * LANES].astype(F32)
            y_ref[rows, lanes] = (bc * c3 * gate_c).astype(BF16)

            acc = jnp.broadcast_to(db_ref[:, lanes], (CONV_ROWS, LANES))
            for k in range(CONF_K):
                off = halo - (CONF_K - 1) + k + r0
                acc = acc + dw_ref[k:k + 1, lanes] * dbuf[c, off:off + CONV_ROWS, :]
            cbuf[slot, :, lanes] = acc

        d = cbuf[slot]
        mu = jnp.mean(d, axis=-1, keepdims=True)
        dc = d - mu
        var = jnp.mean(dc * dc, axis=-1, keepdims=True)
        dn = _silu(dc * lax.rsqrt(var + EPS) * cg_ref[...] + cb_ref[...])
        gate_d = _silu(p_ref[rows, 6 * w:7 * w].astype(F32))
        y_ref[rows, w:] = (dn * gate_d).astype(BF16)
        if (r0 + CONV_ROWS) % OUT_ROWS == 0:
            _outproj_residual(y_ref, wo_ref, ln_ref, x_ref, o_ref, slice(r0 + CONV_ROWS - OUT_ROWS, r0 + CONV_ROWS))


def _odd_tail(p, x2d, sconv_w, dconv_w, dconv_b, cnorm_g, cnorm_b, w_out, ln_post, seq, width):
    m, d = x2d.shape
    n_in = p.shape[1]
    tm, halo = ODD_TM, CONV_HALO
    assert seq % tm == 0 and tm % halo == 0 and tm % OUT_ROWS == 0 and OUT_ROWS % CONV_ROWS == 0
    assert halo >= CONF_K - 1
    assert d == 2 * width and n_in == 7 * width and width % LANES == 0
    assert sconv_w.shape == (SCONV_K, width) and dconv_w.shape == (CONF_K, width)
    const = lambda i: (0, 0)
    return pl.pallas_call(
        functools.partial(_odd_tail_kernel, seq=seq, tm=tm, width=width),
        grid=(m // tm,),
        in_specs=[
            pl.BlockSpec((tm, n_in), lambda i: (i, 0)),
            pl.BlockSpec((halo, n_in), lambda i: (jnp.maximum(i * (tm // halo) - 1, 0), 0)),
            pl.BlockSpec((tm, d), lambda i: (i, 0)),
            _resident((SCONV_K, width), const),
            _resident((CONF_K, width), const),
            _resident((1, width), const),
            _resident((1, width), const),
            _resident((1, width), const),
            _resident((d, d), const),
            _resident((1, d), const),
        ],
        out_specs=pl.BlockSpec((tm, d), lambda i: (i, 0)),
        out_shape=jax.ShapeDtypeStruct((m, d), F32),
        scratch_shapes=[
            pltpu.VMEM((width // LANES, halo + tm, LANES), F32),
            pltpu.VMEM((width // LANES, halo + tm, LANES), F32),
            pltpu.VMEM((2, CONV_ROWS, width), F32),
            pltpu.VMEM((tm, d), BF16),
        ],
        compiler_params=pltpu.CompilerParams(
            dimension_semantics=("parallel",), vmem_limit_bytes=VMEM_LIMIT_BYTES),
        name="odd_tail",
    )(p, p, x2d, sconv_w, dconv_w, dconv_b.reshape(1, width), cnorm_g.reshape(1, width),
      cnorm_b.reshape(1, width), w_out, ln_post.reshape(1, d))


def kernel(x, ln_pre_even, w_in_even, pool_w, pool_scale, w_out_even, ln_post_even,
           ln_pre_odd, w_in_odd, sconv_w, dconv_w, dconv_b, cnorm_g, cnorm_b, w_out_odd, ln_post_odd):
    batch, seq, d = x.shape
    depth = ln_pre_even.shape[0] + ln_pre_odd.shape[0]
    sb_width = d // 2
    heads = sb_width // SB_HEAD_DIM
    conv_width = d // 2
    q_scale = math.log2(math.e) / math.sqrt(SB_HEAD_DIM)

    xf = x.reshape(batch * seq, d)
    for layer in range(depth):
        i = layer // 2
        if layer % 2 == 0:
            n_in = w_in_even.shape[-1]
            col_scale = jnp.where(jnp.arange(n_in) < sb_width, q_scale, 1.0).astype(F32)
            p = _norm_inproj(xf, ln_pre_even[i], w_in_even[i].astype(BF16), col_scale)
            a = _attention(p, batch, seq, heads)
            xf = _even_tail(a, p, xf, pool_w[i].astype(BF16), pool_scale[i], w_out_even[i].astype(BF16),
                            ln_post_even[i], seq, sb_width)
        else:
            p = _norm_inproj(xf, ln_pre_odd[i], w_in_odd[i].astype(BF16))
            xf = _odd_tail(p, xf, sconv_w[i], dconv_w[i], dconv_b[i], cnorm_g[i], cnorm_b[i],
                           w_out_odd[i].astype(BF16), ln_post_odd[i], seq, conv_width)
    return xf.reshape(batch, seq, d)
```

```python
import functools
import math

import jax
import jax.numpy as jnp
from jax import lax
from jax.experimental import pallas as pl
from jax.experimental.pallas import tpu as pltpu

F32 = jnp.float32
BF16 = jnp.bfloat16
EPS = 1e-6

SB_HEAD_DIM = 128
POOL_WINDOWS = (2, 4, 8, 16)
SCONV_K = 3
CONF_K = 31

VMEM_LIMIT_BYTES = 56 * 1024 * 1024
LANES = 128
MXU_DIM = 256

INPROJ_TM = 1024
INPROJ_TN_MAX = 2048
ATTN_BLOCK = 256
HEAD_GROUPS = 2
EVEN_TM = 512
ODD_TM = 512
POOL_CHUNK = 256
POOL_HALO = 16
CONV_HALO = 32
CONV_ROWS = 128
OUT_ROWS = 256


def _sigmoid(v):
    return 1.0 / (1.0 + jnp.exp(-v))


def _silu(v):
    return v * _sigmoid(v)


def _resident(block_shape, index_map):
    return pl.BlockSpec(block_shape, index_map, pipeline_mode=pl.Buffered(1))


def _norm_inproj_kernel(x_ref, g_ref, w_ref, *rest):
    *maybe_scale_ref, o_ref, h_ref = rest

    @pl.when(pl.program_id(1) == 0)
    def _():
        x = x_ref[...]
        ms = jnp.mean(x * x, axis=-1, keepdims=True)
        h_ref[...] = (x * lax.rsqrt(ms + EPS) * g_ref[...]).astype(BF16)

    o = jnp.dot(h_ref[...], w_ref[...], preferred_element_type=F32)
    if maybe_scale_ref:
        o = o * maybe_scale_ref[0][...]
    o_ref[...] = o.astype(o_ref.dtype)


def _norm_inproj(x2d, gain, w, col_scale=None):
    m, d = x2d.shape
    n = w.shape[1]
    tm = INPROJ_TM
    tn = max(t for t in range(MXU_DIM, INPROJ_TN_MAX + 1, MXU_DIM) if n % t == 0)
    assert m % tm == 0
    in_specs = [
        pl.BlockSpec((tm, d), lambda i, j: (i, 0)),
        _resident((1, d), lambda i, j: (0, 0)),
        pl.BlockSpec((d, tn), lambda i, j: (0, j)),
    ]
    operands = [x2d, gain.reshape(1, d), w]
    if col_scale is not None:
        in_specs.append(pl.BlockSpec((1, tn), lambda i, j: (0, j)))
        operands.append(col_scale.reshape(1, n))
    return pl.pallas_call(
        _norm_inproj_kernel,
        grid=(m // tm, n // tn),
        in_specs=in_specs,
        out_specs=pl.BlockSpec((tm, tn), lambda i, j: (i, j)),
        out_shape=jax.ShapeDtypeStruct((m, n), BF16),
        scratch_shapes=[pltpu.VMEM((tm, d), BF16)],
        compiler_params=pltpu.CompilerParams(
            dimension_semantics=("parallel", "arbitrary"), vmem_limit_bytes=VMEM_LIMIT_BYTES),
        name="norm_inproj",
    )(*operands)


def _attn_kernel(q_ref, k_ref, v_ref, o_ref, acc_ref, run_ref, *, seq, blk, heads):
    dh = SB_HEAD_DIM
    row = lax.broadcasted_iota(jnp.int32, (blk, blk), 0)
    col = lax.broadcasted_iota(jnp.int32, (blk, blk), 1)
    later_keys = jnp.where(row > col, 1.0, 0.0).astype(BF16)
    causal = col < row

    def block(q0, k0, masked):
        group = heads // HEAD_GROUPS
        log_beta, stay_terms, later = {}, {}, {}
        for g in range(HEAD_GROUPS):
            group_heads = range(g * group, (g + 1) * group)
            for h in group_heads:
                lanes = slice(h * dh, (h + 1) * dh)
                q = q_ref[pl.ds(q0, blk), lanes]
                kb = k_ref[pl.ds(k0, blk), lanes]
                z = lax.dot_general(q, kb, (((1,), (1,)), ((), ())), preferred_element_type=F32)
                tail = jnp.log2(1.0 + jnp.exp2(-jnp.abs(z)))
                lb = jnp.minimum(z, 0.0) - tail
                sp = z - lb
                log_beta[h] = lb
                stay_terms[h] = jnp.where(causal, sp, 0.0) if masked else sp
            stacked = jnp.concatenate([stay_terms[h].astype(BF16) for h in group_heads], axis=0)
            sums = jnp.dot(stacked, later_keys, preferred_element_type=F32)
            for n, h in enumerate(group_heads):
                later[h] = sums[n * blk:(n + 1) * blk]
        for h in range(heads):
            lanes = slice(h * dh, (h + 1) * dh)
            run = run_ref[:, lanes]
            w = jnp.exp2(log_beta[h] - later[h])
            if masked:
                w = jnp.where(causal, w, 0.0)
            vb = v_ref[pl.ds(k0, blk), lanes]
            pv = jnp.dot(w.astype(BF16), vb, preferred_element_type=F32)
            acc_ref[:, lanes] += jnp.exp2(-run) * pv
            run_ref[:, lanes] = run + jnp.sum(stay_terms[h], axis=-1, keepdims=True)

    def q_body(qi, carry):
        q0 = pl.multiple_of(qi * blk, blk)
        acc_ref[...] = jnp.zeros_like(acc_ref)
        run_ref[...] = jnp.zeros_like(run_ref)
        block(q0, q0, True)

        def k_body(n, c):
            k0 = pl.multiple_of((qi - 1 - n) * blk, blk)
            block(q0, k0, False)
            return c

        lax.fori_loop(0, qi, k_body, 0)
        o_ref[pl.ds(q0, blk), :] = acc_ref[...].astype(o_ref.dtype)
        return carry

    lax.fori_loop(0, seq // blk, q_body, 0)


def _attention(p, batch, seq, heads):
    blk = ATTN_BLOCK
    assert seq % blk == 0 and SB_HEAD_DIM == LANES
    width = heads * SB_HEAD_DIM
    return pl.pallas_call(
        functools.partial(_attn_kernel, seq=seq, blk=blk, heads=heads),
        grid=(batch,),
        in_specs=[
            pl.BlockSpec((seq, width), lambda b: (b, 0)),
            pl.BlockSpec((seq, width), lambda b: (b, 1)),
            pl.BlockSpec((seq, width), lambda b: (b, 2)),
        ],
        out_specs=pl.BlockSpec((seq, width), lambda b: (b, 0)),
        out_shape=jax.ShapeDtypeStruct((batch * seq, width), BF16),
        scratch_shapes=[pltpu.VMEM((blk, width), F32), pltpu.VMEM((blk, width), F32)],
        compiler_params=pltpu.CompilerParams(
            dimension_semantics=("parallel",), vmem_limit_bytes=VMEM_LIMIT_BYTES),
        name="stickbreak_attention",
    )(p, p, p)


def _outproj_residual(y_ref, wo_ref, ln_ref, x_ref, o_ref, rows):
    o = jnp.dot(y_ref[rows, :], wo_ref[...], preferred_element_type=F32)
    ms = jnp.mean(o * o, axis=-1, keepdims=True)
    o_ref[rows, :] = x_ref[rows, :] + o * lax.rsqrt(ms + EPS) * ln_ref[...]


def _even_tail_kernel(a_ref, u_ref, uh_ref, g_ref, x_ref, pw_ref, ps_ref, wo_ref, ln_ref, *rest,
                      seq, tm, sb_width, group_dim, n_cast):
    cast_in, (o_ref, *cast_out), y_ref = rest[:n_cast], rest[n_cast:2 * n_cast + 1], rest[-1]
    for src, dst in zip(cast_in, cast_out):
        dst[...] = src[...].astype(BF16)

    i = pl.program_id(0)
    pos0 = (i * tm) % seq
    halo, chunk = POOL_HALO, POOL_CHUNK
    u_hist0 = jnp.where(pos0 == 0, jnp.zeros((halo, u_ref.shape[1]), BF16), uh_ref[...])

    lag = (lax.broadcasted_iota(jnp.int32, (chunk, chunk), 0)
           - lax.broadcasted_iota(jnp.int32, (chunk, chunk), 1))
    lag_hist = (lax.broadcasted_iota(jnp.int32, (halo, halo), 0) + halo
                - lax.broadcasted_iota(jnp.int32, (halo, halo), 1))

    bands = [jnp.where((lag >= 0) & (lag < win), 1.0, 0.0).astype(BF16) for win in POOL_WINDOWS]
    bands_hist = [jnp.where(lag_hist < win, 1.0, 0.0).astype(BF16) for win in POOL_WINDOWS]

    for r0 in range(0, tm, chunk):
        rows = slice(r0, r0 + chunk)
        y_ref[rows, :sb_width] = (a_ref[rows, :].astype(F32)
                                  * _silu(g_ref[rows, :sb_width].astype(F32))).astype(BF16)
        pos = pos0 + r0 + lax.broadcasted_iota(jnp.int32, (chunk, 1), 0)
        for gi, win in enumerate(POOL_WINDOWS):
            lo_c, hi_c = gi * group_dim, (gi + 1) * group_dim
            ug = u_ref[rows, lo_c:hi_c]
            hist = u_hist0[:, lo_c:hi_c] if r0 == 0 else u_ref[r0 - halo:r0, lo_c:hi_c]
            wsum = jnp.dot(bands[gi], ug, preferred_element_type=F32)
            head = wsum[:halo] + jnp.dot(bands_hist[gi], hist, preferred_element_type=F32)
            wsum = jnp.concatenate([head, wsum[halo:]], axis=0)
            inv_count = 1.0 / jnp.minimum(win, pos + 1).astype(F32)
            pooled = wsum * inv_count - ug.astype(F32)
            yg = jnp.dot(pooled.astype(BF16), pw_ref[gi], preferred_element_type=F32) * ps_ref[:, lo_c:hi_c]
            gate = _silu(g_ref[rows, sb_width + lo_c:sb_width + hi_c].astype(F32))
            y_ref[rows, sb_width + lo_c:sb_width + hi_c] = (yg * gate).astype(BF16)
        _outproj_residual(y_ref, wo_ref, ln_ref, x_ref, o_ref, rows)


def _even_tail(a, p, x2d, pool_w, pool_scale, w_out, ln_post, seq, sb_width, cast_weights=()):
    m, d = x2d.shape
    tm, halo = EVEN_TM, POOL_HALO
    steps = m // tm
    slab = lambda wgt: (wgt.shape[0] // steps, wgt.shape[1])
    assert all(wgt.shape[0] % (steps * 16) == 0 for wgt in cast_weights)
    pool_width = d - sb_width
    groups, group_dim, _ = pool_w.shape
    assert seq % tm == 0 and tm % POOL_CHUNK == 0 and POOL_CHUNK % halo == 0 and groups == len(POOL_WINDOWS)
    assert halo >= max(POOL_WINDOWS) - 1 and pool_width == sb_width and d % pool_width == 0
    u_col = (3 * sb_width) // pool_width
    g_col = (3 * sb_width + pool_width) // d
    assert u_col * pool_width == 3 * sb_width and g_col * d == 3 * sb_width + pool_width
    const = lambda i: (0, 0)
    row_block = lambda i: (i, 0)
    return pl.pallas_call(
        functools.partial(_even_tail_kernel, seq=seq, tm=tm, sb_width=sb_width, group_dim=group_dim,
                          n_cast=len(cast_weights)),
        grid=(steps,),
        in_specs=[
            pl.BlockSpec((tm, sb_width), row_block),
            pl.BlockSpec((tm, pool_width), lambda i: (i, u_col)),
            pl.BlockSpec((halo, pool_width), lambda i: (jnp.maximum(i * (tm // halo) - 1, 0), u_col)),
            pl.BlockSpec((tm, d), lambda i: (i, g_col)),
            pl.BlockSpec((tm, d), row_block),
            _resident((groups, group_dim, group_dim), lambda i: (0, 0, 0)),
            _resident((1, pool_width), const),
            _resident((d, d), const),
            _resident((1, d), const),
        ] + [pl.BlockSpec(slab(wgt), row_block) for wgt in cast_weights],
        out_specs=[pl.BlockSpec((tm, d), row_block)] + [pl.BlockSpec(slab(wgt), row_block) for wgt in cast_weights],
        out_shape=([jax.ShapeDtypeStruct((m, d), F32)]
                   + [jax.ShapeDtypeStruct(wgt.shape, BF16) for wgt in cast_weights]),
        scratch_shapes=[pltpu.VMEM((tm, d), BF16)],
        compiler_params=pltpu.CompilerParams(
            dimension_semantics=("parallel",), vmem_limit_bytes=VMEM_LIMIT_BYTES),
        name="even_tail",
    )(a, p, p, p, x2d, pool_w, pool_scale.reshape(1, pool_width), w_out, ln_post.reshape(1, d), *cast_weights)


def _odd_tail_kernel(p_ref, ph_ref, x_ref, sw_ref, dw_ref, db_ref, cg_ref, cb_ref, wo_ref, ln_ref, o_ref,
                     zbuf, dbuf, cbuf, y_ref, *, seq, tm, width):
    i = pl.program_id(0)
    halo = CONV_HALO
    w = width
    planes = w // LANES
    keep = jnp.where((i * tm) % seq == 0, 0.0, 1.0).astype(F32)

    def piece(ref, n, c):
        return ref[:, n * w + c * LANES:n * w + (c + 1) * LANES].astype(F32)

    for c in range(planes):
        zbuf[c, :halo, :] = piece(ph_ref, 2, c) * piece(ph_ref, 0, c) * keep
        zbuf[c, halo:, :] = piece(p_ref, 2, c) * piece(p_ref, 0, c)
        dbuf[c, :halo, :] = piece(ph_ref, 3, c) * _sigmoid(piece(ph_ref, 4, c)) * keep
        dbuf[c, halo:, :] = piece(p_ref, 3, c) * _sigmoid(piece(p_ref, 4, c))

    for ci, r0 in enumerate(range(0, tm, CONV_ROWS)):
        rows = slice(r0, r0 + CONV_ROWS)
        slot = ci % cbuf.shape[0]
        for c in range(planes):
            lanes = slice(c * LANES, (c + 1) * LANES)
            c3 = jnp.zeros((CONV_ROWS, LANES), F32)
            for k in range(SCONV_K):
                off = halo - (SCONV_K - 1) + k + r0
                c3 = c3 + sw_ref[k:k + 1, lanes] * zbuf[c, off:off + CONV_ROWS, :]
            gate_c = _silu(p_ref[rows, 5 * w + c * LANES:5 * w + (c + 1) * LANES].astype(F32))
            bc = p_ref[rows, w + c * LANES:w + (c + 1) * LANES].astype(F32)
            y_ref[rows, lanes] = (bc * c3 * gate_c).astype(BF16)

            acc = jnp.broadcast_to(db_ref[:, lanes], (CONV_ROWS, LANES))
            for k in range(CONF_K):
                off = halo - (CONF_K - 1) + k + r0
                acc = acc + dw_ref[k:k + 1, lanes] * dbuf[c, off:off + CONV_ROWS, :]
            cbuf[slot, :, lanes] = acc

        d = cbuf[slot]
        mu = jnp.mean(d, axis=-1, keepdims=True)
        dc = d - mu
        var = jnp.mean(dc * dc, axis=-1, keepdims=True)
        dn = _silu(dc * lax.rsqrt(var + EPS) * cg_ref[...] + cb_ref[...])
        gate_d = _silu(p_ref[rows, 6 * w:7 * w].astype(F32))
        y_ref[rows, w:] = (dn * gate_d).astype(BF16)
        if (r0 + CONV_ROWS) % OUT_ROWS == 0:
            _outproj_residual(y_ref, wo_ref, ln_ref, x_ref, o_ref, slice(r0 + CONV_ROWS - OUT_ROWS, r0 + CONV_ROWS))


def _odd_tail(p, x2d, sconv_w, dconv_w, dconv_b, cnorm_g, cnorm_b, w_out, ln_post, seq, width):
    m, d = x2d.shape
    n_in = p.shape[1]
    tm, halo = ODD_TM, CONV_HALO
    assert seq % tm == 0 and tm % halo == 0 and tm % OUT_ROWS == 0 and OUT_ROWS % CONV_ROWS == 0
    assert halo >= CONF_K - 1
    assert d == 2 * width and n_in == 7 * width and width % LANES == 0
    assert sconv_w.shape == (SCONV_K, width) and dconv_w.shape == (CONF_K, width)
    const = lambda i: (0, 0)
    return pl.pallas_call(
        functools.partial(_odd_tail_kernel, seq=seq, tm=tm, width=width),
        grid=(m // tm,),
        in_specs=[
            pl.BlockSpec((tm, n_in), lambda i: (i, 0)),
            pl.BlockSpec((halo, n_in), lambda i: (jnp.maximum(i * (tm // halo) - 1, 0), 0)),
            pl.BlockSpec((tm, d), lambda i: (i, 0)),
            _resident((SCONV_K, width), const),
            _resident((CONF_K, width), const),
            _resident((1, width), const),
            _resident((1, width), const),
            _resident((1, width), const),
            _resident((d, d), const),
            _resident((1, d), const),
        ],
        out_specs=pl.BlockSpec((tm, d), lambda i: (i, 0)),
        out_shape=jax.ShapeDtypeStruct((m, d), F32),
        scratch_shapes=[
            pltpu.VMEM((width // LANES, halo + tm, LANES), F32),
            pltpu.VMEM((width // LANES, halo + tm, LANES), F32),
            pltpu.VMEM((2, CONV_ROWS, width), F32),
            pltpu.VMEM((tm, d), BF16),
        ],
        compiler_params=pltpu.CompilerParams(
            dimension_semantics=("parallel",), vmem_limit_bytes=VMEM_LIMIT_BYTES),
        name="odd_tail",
    )(p, p, x2d, sconv_w, dconv_w, dconv_b.reshape(1, width), cnorm_g.reshape(1, width),
      cnorm_b.reshape(1, width), w_out, ln_post.reshape(1, d))


def kernel(x, ln_pre_even, w_in_even, pool_w, pool_scale, w_out_even, ln_post_even,
           ln_pre_odd, w_in_odd, sconv_w, dconv_w, dconv_b, cnorm_g, cnorm_b, w_out_odd, ln_post_odd):
    batch, seq, d = x.shape
    depth = ln_pre_even.shape[0] + ln_pre_odd.shape[0]
    sb_width = d // 2
    heads = sb_width // SB_HEAD_DIM
    conv_width = d // 2
    q_scale = math.log2(math.e) / math.sqrt(SB_HEAD_DIM)

    xf = x.reshape(batch * seq, d)
    odd_weights = None
    for layer in range(depth):
        i = layer // 2
        if layer % 2 == 0:
            n_in = w_in_even.shape[-1]
            col_scale = jnp.where(jnp.arange(n_in) < sb_width, q_scale, 1.0).astype(F32)
            p = _norm_inproj(xf, ln_pre_even[i], w_in_even[i].astype(BF16), col_scale)
            a = _attention(p, batch, seq, heads)
            coming = (w_in_odd[i], w_out_odd[i]) if layer + 1 < depth else ()
            xf, *rounded = _even_tail(a, p, xf, pool_w[i].astype(BF16), pool_scale[i], w_out_even[i].astype(BF16),
                                      ln_post_even[i], seq, sb_width, coming)
            odd_weights = rounded or None
        else:
            w_in, w_out = odd_weights or (w_in_odd[i].astype(BF16), w_out_odd[i].astype(BF16))
            p = _norm_inproj(xf, ln_pre_odd[i], w_in)
            xf = _odd_tail(p, xf, sconv_w[i], dconv_w[i], dconv_b[i], cnorm_g[i], cnorm_b[i],
                           w_out, ln_post_odd[i], seq, conv_width)
    return xf.reshape(batch, seq, d)
```

```python
import functools
import math

import jax
import jax.numpy as jnp
from jax import lax
from jax.experimental import pallas as pl
from jax.experimental.pallas import tpu as pltpu

F32 = jnp.float32
BF16 = jnp.bfloat16
EPS = 1e-6

SB_HEAD_DIM = 128
POOL_WINDOWS = (2, 4, 8, 16)
SCONV_K = 3
CONF_K = 31

VMEM_LIMIT_BYTES = 56 * 1024 * 1024
LANES = 128
MXU_DIM = 256

INPROJ_TM = 1024
INPROJ_TN_MAX = 2048
ATTN_BLOCK = 256
HEAD_GROUPS = 2
EVEN_TM = 512
ODD_TM = 512
POOL_CHUNK = 256
POOL_HALO = 16
CONV_HALO = 32
CONV_ROWS = 128
OUT_ROWS = 256


def _sigmoid(v):
    return 1.0 / (1.0 + jnp.exp(-v))


def _silu(v):
    return v * _sigmoid(v)


def _resident(block_shape, index_map):
    return pl.BlockSpec(block_shape, index_map, pipeline_mode=pl.Buffered(1))


def _norm_inproj_kernel(x_ref, g_ref, w_ref, *rest):
    *maybe_scale_ref, o_ref, h_ref = rest

    @pl.when(pl.program_id(1) == 0)
    def _():
        x = x_ref[...]
        ms = jnp.mean(x * x, axis=-1, keepdims=True)
        h_ref[...] = (x * lax.rsqrt(ms + EPS) * g_ref[...]).astype(BF16)

    o = jnp.dot(h_ref[...], w_ref[...], preferred_element_type=F32)
    if maybe_scale_ref:
        o = o * maybe_scale_ref[0][...]
    o_ref[...] = o.astype(o_ref.dtype)


def _norm_inproj(x2d, gain, w, col_scale=None):
    m, d = x2d.shape
    n = w.shape[1]
    tm = INPROJ_TM
    tn = max(t for t in range(MXU_DIM, INPROJ_TN_MAX + 1, MXU_DIM) if n % t == 0)
    assert m % tm == 0
    in_specs = [
        pl.BlockSpec((tm, d), lambda i, j: (i, 0)),
        _resident((1, d), lambda i, j: (0, 0)),
        pl.BlockSpec((d, tn), lambda i, j: (0, j)),
    ]
    operands = [x2d, gain.reshape(1, d), w]
    if col_scale is not None:
        in_specs.append(pl.BlockSpec((1, tn), lambda i, j: (0, j)))
        operands.append(col_scale.reshape(1, n))
    return pl.pallas_call(
        _norm_inproj_kernel,
        grid=(m // tm, n // tn),
        in_specs=in_specs,
        out_specs=pl.BlockSpec((tm, tn), lambda i, j: (i, j)),
        out_shape=jax.ShapeDtypeStruct((m, n), BF16),
        scratch_shapes=[pltpu.VMEM((tm, d), BF16)],
        compiler_params=pltpu.CompilerParams(
            dimension_semantics=("parallel", "arbitrary"), vmem_limit_bytes=VMEM_LIMIT_BYTES),
        name="norm_inproj",
    )(*operands)


def _attn_kernel(q_ref, k_ref, v_ref, o_ref, acc_ref, run_ref, *, seq, blk, heads):
    dh = SB_HEAD_DIM
    row = lax.broadcasted_iota(jnp.int32, (blk, blk), 0)
    col = lax.broadcasted_iota(jnp.int32, (blk, blk), 1)
    later_keys = jnp.where(row > col, 1.0, 0.0).astype(BF16)
    causal = col < row

    def block(q0, k0, masked):
        group = heads // HEAD_GROUPS
        log_beta, stay_terms, later = {}, {}, {}
        for g in range(HEAD_GROUPS):
            group_heads = range(g * group, (g + 1) * group)
            for h in group_heads:
                lanes = slice(h * dh, (h + 1) * dh)
                q = q_ref[pl.ds(q0, blk), lanes]
                kb = k_ref[pl.ds(k0, blk), lanes]
                z = lax.dot_general(q, kb, (((1,), (1,)), ((), ())), preferred_element_type=F32)
                tail = jnp.log2(1.0 + jnp.exp2(-jnp.abs(z)))
                lb = jnp.minimum(z, 0.0) - tail
                sp = z - lb
                log_beta[h] = lb
                stay_terms[h] = jnp.where(causal, sp, 0.0) if masked else sp
            stacked = jnp.concatenate([stay_terms[h].astype(BF16) for h in group_heads], axis=0)
            sums = jnp.dot(stacked, later_keys, preferred_element_type=F32)
            for n, h in enumerate(group_heads):
                later[h] = sums[n * blk:(n + 1) * blk]
        for h in range(heads):
            lanes = slice(h * dh, (h + 1) * dh)
            run = run_ref[:, lanes]
            w = jnp.exp2(log_beta[h] - later[h])
            if masked:
                w = jnp.where(causal, w, 0.0)
            vb = v_ref[pl.ds(k0, blk), lanes]
            pv = jnp.dot(w.astype(BF16), vb, preferred_element_type=F32)
            acc_ref[:, lanes] += jnp.exp2(-run) * pv
            run_ref[:, lanes] = run + jnp.sum(stay_terms[h], axis=-1, keepdims=True)

    def q_body(qi, carry):
        q0 = pl.multiple_of(qi * blk, blk)
        acc_ref[...] = jnp.zeros_like(acc_ref)
        run_ref[...] = jnp.zeros_like(run_ref)
        block(q0, q0, True)

        def k_body(n, c):
            k0 = pl.multiple_of((qi - 1 - n) * blk, blk)
            block(q0, k0, False)
            return c

        lax.fori_loop(0, qi, k_body, 0)
        o_ref[pl.ds(q0, blk), :] = acc_ref[...].astype(o_ref.dtype)
        return carry

    lax.fori_loop(0, seq // blk, q_body, 0)


def _attention(p, batch, seq, heads):
    blk = ATTN_BLOCK
    assert seq % blk == 0 and SB_HEAD_DIM == LANES
    width = heads * SB_HEAD_DIM
    return pl.pallas_call(
        functools.partial(_attn_kernel, seq=seq, blk=blk, heads=heads),
        grid=(batch,),
        in_specs=[
            pl.BlockSpec((seq, width), lambda b: (b, 0)),
            pl.BlockSpec((seq, width), lambda b: (b, 1)),
            pl.BlockSpec((seq, width), lambda b: (b, 2)),
        ],
        out_specs=pl.BlockSpec((seq, width), lambda b: (b, 0)),
        out_shape=jax.ShapeDtypeStruct((batch * seq, width), BF16),
        scratch_shapes=[pltpu.VMEM((blk, width), F32), pltpu.VMEM((blk, width), F32)],
        compiler_params=pltpu.CompilerParams(
            dimension_semantics=("parallel",), vmem_limit_bytes=VMEM_LIMIT_BYTES),
        name="stickbreak_attention",
    )(p, p, p)


def _outproj_residual(y_ref, wo_ref, ln_ref, x_ref, o_ref, rows):
    o = jnp.dot(y_ref[rows, :], wo_ref[...], preferred_element_type=F32)
    ms = jnp.mean(o * o, axis=-1, keepdims=True)
    o_ref[rows, :] = x_ref[rows, :] + o * lax.rsqrt(ms + EPS) * ln_ref[...]


def _even_tail_kernel(a_ref, u_ref, uh_ref, g_ref, x_ref, pw_ref, ps_ref, wo_ref, ln_ref, *rest,
                      seq, tm, sb_width, group_dim, n_cast):
    cast_in, (o_ref, *cast_out), y_ref = rest[:n_cast], rest[n_cast:2 * n_cast + 1], rest[-1]
    for src, dst in zip(cast_in, cast_out):
        dst[...] = src[...].astype(BF16)

    i = pl.program_id(0)
    pos0 = (i * tm) % seq
    halo, chunk = POOL_HALO, POOL_CHUNK
    u_hist0 = jnp.where(pos0 == 0, jnp.zeros((halo, u_ref.shape[1]), BF16), uh_ref[...])

    lag = (lax.broadcasted_iota(jnp.int32, (chunk, chunk), 0)
           - lax.broadcasted_iota(jnp.int32, (chunk, chunk), 1))
    lag_hist = (lax.broadcasted_iota(jnp.int32, (halo, halo), 0) + halo
                - lax.broadcasted_iota(jnp.int32, (halo, halo), 1))

    bands = [jnp.where((lag >= 0) & (lag < win), 1.0, 0.0).astype(BF16) for win in POOL_WINDOWS]
    bands_hist = [jnp.where(lag_hist < win, 1.0, 0.0).astype(BF16) for win in POOL_WINDOWS]

    for r0 in range(0, tm, chunk):
        rows = slice(r0, r0 + chunk)
        y_ref[rows, :sb_width] = (a_ref[rows, :].astype(F32)
                                  * _silu(g_ref[rows, :sb_width].astype(F32))).astype(BF16)
        pos = pos0 + r0 + lax.broadcasted_iota(jnp.int32, (chunk, 1), 0)
        group_cols = [slice(gi * group_dim, (gi + 1) * group_dim) for gi in range(len(POOL_WINDOWS))]
        ugs = [u_ref[rows, cols] for cols in group_cols]
        wsums = []
        for gi, cols in enumerate(group_cols):
            hist = u_hist0[:, cols] if r0 == 0 else u_ref[r0 - halo:r0, cols]
            wsum = jnp.dot(bands[gi], ugs[gi], preferred_element_type=F32)
            head = wsum[:halo] + jnp.dot(bands_hist[gi], hist, preferred_element_type=F32)
            wsums.append(jnp.concatenate([head, wsum[halo:]], axis=0))
        pooled = []
        for gi, win in enumerate(POOL_WINDOWS):
            inv_count = 1.0 / jnp.minimum(win, pos + 1).astype(F32)
            pooled.append((wsums[gi] * inv_count - ugs[gi].astype(F32)).astype(BF16))
        ygs = [jnp.dot(pooled[gi], pw_ref[gi], preferred_element_type=F32) for gi in range(len(group_cols))]
        for gi, cols in enumerate(group_cols):
            out_cols = slice(sb_width + cols.start, sb_width + cols.stop)
            gate = _silu(g_ref[rows, out_cols].astype(F32))
            y_ref[rows, out_cols] = (ygs[gi] * ps_ref[:, cols] * gate).astype(BF16)
        _outproj_residual(y_ref, wo_ref, ln_ref, x_ref, o_ref, rows)


def _even_tail(a, p, x2d, pool_w, pool_scale, w_out, ln_post, seq, sb_width, cast_weights=()):
    m, d = x2d.shape
    tm, halo = EVEN_TM, POOL_HALO
    steps = m // tm
    slab = lambda wgt: (wgt.shape[0] // steps, wgt.shape[1])
    assert all(wgt.shape[0] % (steps * 16) == 0 for wgt in cast_weights)
    pool_width = d - sb_width
    groups, group_dim, _ = pool_w.shape
    assert seq % tm == 0 and tm % POOL_CHUNK == 0 and POOL_CHUNK % halo == 0 and groups == len(POOL_WINDOWS)
    assert halo >= max(POOL_WINDOWS) - 1 and pool_width == sb_width and d % pool_width == 0
    u_col = (3 * sb_width) // pool_width
    g_col = (3 * sb_width + pool_width) // d
    assert u_col * pool_width == 3 * sb_width and g_col * d == 3 * sb_width + pool_width
    const = lambda i: (0, 0)
    row_block = lambda i: (i, 0)
    return pl.pallas_call(
        functools.partial(_even_tail_kernel, seq=seq, tm=tm, sb_width=sb_width, group_dim=group_dim,
                          n_cast=len(cast_weights)),
        grid=(steps,),
        in_specs=[
            pl.BlockSpec((tm, sb_width), row_block),
            pl.BlockSpec((tm, pool_width), lambda i: (i, u_col)),
            pl.BlockSpec((halo, pool_width), lambda i: (jnp.maximum(i * (tm // halo) - 1, 0), u_col)),
            pl.BlockSpec((tm, d), lambda i: (i, g_col)),
            pl.BlockSpec((tm, d), row_block),
            _resident((groups, group_dim, group_dim), lambda i: (0, 0, 0)),
            _resident((1, pool_width), const),
            _resident((d, d), const),
            _resident((1, d), const),
        ] + [pl.BlockSpec(slab(wgt), row_block) for wgt in cast_weights],
        out_specs=[pl.BlockSpec((tm, d), row_block)] + [pl.BlockSpec(slab(wgt), row_block) for wgt in cast_weights],
        out_shape=([jax.ShapeDtypeStruct((m, d), F32)]
                   + [jax.ShapeDtypeStruct(wgt.shape, BF16) for wgt in cast_weights]),
        scratch_shapes=[pltpu.VMEM((tm, d), BF16)],
        compiler_params=pltpu.CompilerParams(
            dimension_semantics=("parallel",), vmem_limit_bytes=VMEM_LIMIT_BYTES),
        name="even_tail",
    )(a, p, p, p, x2d, pool_w, pool_scale.reshape(1, pool_width), w_out, ln_post.reshape(1, d), *cast_weights)


def _odd_tail_kernel(p_ref, ph_ref, x_ref, sw_ref, dw_ref, db_ref, cg_ref, cb_ref, wo_ref, ln_ref, o_ref,
                     zbuf, dbuf, cbuf, y_ref, *, seq, tm, width):
    i = pl.program_id(0)
    halo = CONV_HALO
    w = width
    planes = w // LANES
    keep = jnp.where((i * tm) % seq == 0, 0.0, 1.0).astype(F32)

    def piece(ref, n, c):
        return ref[:, n * w + c * LANES:n * w + (c + 1) * LANES].astype(F32)

    for c in range(planes):
        zbuf[c, :halo, :] = piece(ph_ref, 2, c) * piece(ph_ref, 0, c) * keep
        zbuf[c, halo:, :] = piece(p_ref, 2, c) * piece(p_ref, 0, c)
        dbuf[c, :halo, :] = piece(ph_ref, 3, c) * _sigmoid(piece(ph_ref, 4, c)) * keep
        dbuf[c, halo:, :] = piece(p_ref, 3, c) * _sigmoid(piece(p_ref, 4, c))

    for ci, r0 in enumerate(range(0, tm, CONV_ROWS)):
        rows = slice(r0, r0 + CONV_ROWS)
        slot = ci % cbuf.shape[0]
        for c in range(planes):
            lanes = slice(c * LANES, (c + 1) * LANES)
            c3 = jnp.zeros((CONV_ROWS, LANES), F32)
            for k in range(SCONV_K):
                off = halo - (SCONV_K - 1) + k + r0
                c3 = c3 + sw_ref[k:k + 1, lanes] * zbuf[c, off:off + CONV_ROWS, :]
            gate_c = _silu(p_ref[rows, 5 * w + c * LANES:5 * w + (c + 1) * LANES].astype(F32))
            bc = p_ref[rows, w + c * LANES:w + (c + 1) * LANES].astype(F32)
            y_ref[rows, lanes] = (bc * c3 * gate_c).astype(BF16)

            acc = jnp.broadcast_to(db_ref[:, lanes], (CONV_ROWS, LANES))
            for k in range(CONF_K):
                off = halo - (CONF_K - 1) + k + r0
                acc = acc + dw_ref[k:k + 1, lanes] * dbuf[c, off:off + CONV_ROWS, :]
            cbuf[slot, :, lanes] = acc

        d = cbuf[slot]
        mu = jnp.mean(d, axis=-1, keepdims=True)
        dc = d - mu
        var = jnp.mean(dc * dc, axis=-1, keepdims=True)
        dn = _silu(dc * lax.rsqrt(var + EPS) * cg_ref[...] + cb_ref[...])
        gate_d = _silu(p_ref[rows, 6 * w:7 * w].astype(F32))
        y_ref[rows, w:] = (dn * gate_d).astype(BF16)
        if (r0 + CONV_ROWS) % OUT_ROWS == 0:
            _outproj_residual(y_ref, wo_ref, ln_ref, x_ref, o_ref, slice(r0 + CONV_ROWS - OUT_ROWS, r0 + CONV_ROWS))


def _odd_tail(p, x2d, sconv_w, dconv_w, dconv_b, cnorm_g, cnorm_b, w_out, ln_post, seq, width):
    m, d = x2d.shape
    n_in = p.shape[1]
    tm, halo = ODD_TM, CONV_HALO
    assert seq % tm == 0 and tm % halo == 0 and tm % OUT_ROWS == 0 and OUT_ROWS % CONV_ROWS == 0
    assert halo >= CONF_K - 1
    assert d == 2 * width and n_in == 7 * width and width % LANES == 0
    assert sconv_w.shape == (SCONV_K, width) and dconv_w.shape == (CONF_K, width)
    const = lambda i: (0, 0)
    return pl.pallas_call(
        functools.partial(_odd_tail_kernel, seq=seq, tm=tm, width=width),
        grid=(m // tm,),
        in_specs=[
            pl.BlockSpec((tm, n_in), lambda i: (i, 0)),
            pl.BlockSpec((halo, n_in), lambda i: (jnp.maximum(i * (tm // halo) - 1, 0), 0)),
            pl.BlockSpec((tm, d), lambda i: (i, 0)),
            _resident((SCONV_K, width), const),
            _resident((CONF_K, width), const),
            _resident((1, width), const),
            _resident((1, width), const),
            _resident((1, width), const),
            _resident((d, d), const),
            _resident((1, d), const),
        ],
        out_specs=pl.BlockSpec((tm, d), lambda i: (i, 0)),
        out_shape=jax.ShapeDtypeStruct((m, d), F32),
        scratch_shapes=[
            pltpu.VMEM((width // LANES, halo + tm, LANES), F32),
            pltpu.VMEM((width // LANES, halo + tm, LANES), F32),
            pltpu.VMEM((2, CONV_ROWS, width), F32),
            pltpu.VMEM((tm, d), BF16),
        ],
        compiler_params=pltpu.CompilerParams(
            dimension_semantics=("parallel",), vmem_limit_bytes=VMEM_LIMIT_BYTES),
        name="odd_tail",
    )(p, p, x2d, sconv_w, dconv_w, dconv_b.reshape(1, width), cnorm_g.reshape(1, width),
      cnorm_b.reshape(1, width), w_out, ln_post.reshape(1, d))


def kernel(x, ln_pre_even, w_in_even, pool_w, pool_scale, w_out_even, ln_post_even,
           ln_pre_odd, w_in_odd, sconv_w, dconv_w, dconv_b, cnorm_g, cnorm_b, w_out_odd, ln_post_odd):
    batch, seq, d = x.shape
    depth = ln_pre_even.shape[0] + ln_pre_odd.shape[0]
    sb_width = d // 2
    heads = sb_width // SB_HEAD_DIM
    conv_width = d // 2
    q_scale = math.log2(math.e) / math.sqrt(SB_HEAD_DIM)

    xf = x.reshape(batch * seq, d)
    odd_weights = None
    for layer in range(depth):
        i = layer // 2
        if layer % 2 == 0:
            n_in = w_in_even.shape[-1]
            col_scale = jnp.where(jnp.arange(n_in) < sb_width, q_scale, 1.0).astype(F32)
            p = _norm_inproj(xf, ln_pre_even[i], w_in_even[i].astype(BF16), col_scale)
            a = _attention(p, batch, seq, heads)
            coming = (w_in_odd[i], w_out_odd[i]) if layer + 1 < depth else ()
            xf, *rounded = _even_tail(a, p, xf, pool_w[i].astype(BF16), pool_scale[i], w_out_even[i].astype(BF16),
                                      ln_post_even[i], seq, sb_width, coming)
            odd_weights = rounded or None
        else:
            w_in, w_out = odd_weights or (w_in_odd[i].astype(BF16), w_out_odd[i].astype(BF16))
            p = _norm_inproj(xf, ln_pre_odd[i], w_in)
            xf = _odd_tail(p, xf, sconv_w[i], dconv_w[i], dconv_b[i], cnorm_g[i], cnorm_b[i],
                           w_out, ln_post_odd[i], seq, conv_width)
    return xf.reshape(batch, seq, d)
```

```python
import functools
import math

import jax
import jax.numpy as jnp
from jax import lax
from jax.experimental import pallas as pl
from jax.experimental.pallas import tpu as pltpu

F32 = jnp.float32
BF16 = jnp.bfloat16
EPS = 1e-6

SB_HEAD_DIM = 128
POOL_WINDOWS = (2, 4, 8, 16)
SCONV_K = 3
CONF_K = 31

VMEM_LIMIT_BYTES = 56 * 1024 * 1024
LANES = 128
MXU_DIM = 256

INPROJ_TM = 1024
INPROJ_TN_MAX = 2048
ATTN_BLOCK = 256
HEAD_GROUPS = 2
EVEN_TM = 512
ODD_TM = 512
POOL_CHUNK = 256
POOL_HALO = 16
CONV_HALO = 32
CONV_ROWS = 128
OUT_ROWS = 256


def _sigmoid(v):
    return 1.0 / (1.0 + jnp.exp(-v))


def _silu(v):
    return v * _sigmoid(v)


def _resident(block_shape, index_map):
    return pl.BlockSpec(block_shape, index_map, pipeline_mode=pl.Buffered(1))


def _norm_inproj_kernel(x_ref, g_ref, w_ref, *rest, scaled, n_cast):
    h_ref = rest[-1]
    maybe_scale_ref, rest = rest[:scaled], rest[scaled:-1]
    cast_in, (o_ref, *cast_out) = rest[:n_cast], rest[n_cast:]

    @pl.when(pl.program_id(1) == 0)
    def _():
        x = x_ref[...]
        ms = jnp.mean(x * x, axis=-1, keepdims=True)
        h_ref[...] = (x * lax.rsqrt(ms + EPS) * g_ref[...]).astype(BF16)
        for src, dst in zip(cast_in, cast_out):
            dst[...] = src[...].astype(BF16)

    o = jnp.dot(h_ref[...], w_ref[...], preferred_element_type=F32)
    if maybe_scale_ref:
        o = o * maybe_scale_ref[0][...]
    o_ref[...] = o.astype(o_ref.dtype)


def _norm_inproj(x2d, gain, w, col_scale=None, cast_weights=()):
    m, d = x2d.shape
    n = w.shape[1]
    tm = INPROJ_TM
    tn = max(t for t in range(MXU_DIM, INPROJ_TN_MAX + 1, MXU_DIM) if n % t == 0)
    assert m % tm == 0
    steps = m // tm
    slab = lambda wgt: (wgt.shape[0] // steps,) + wgt.shape[1:]
    slab_block = lambda wgt: pl.BlockSpec(slab(wgt), lambda i, j: (i,) + (0,) * (wgt.ndim - 1))
    assert all(wgt.shape[0] % steps == 0 and (slab(wgt)[-2] % 16 == 0) for wgt in cast_weights)
    in_specs = [
        pl.BlockSpec((tm, d), lambda i, j: (i, 0)),
        _resident((1, d), lambda i, j: (0, 0)),
        pl.BlockSpec((d, tn), lambda i, j: (0, j)),
    ]
    operands = [x2d, gain.reshape(1, d), w]
    if col_scale is not None:
        in_specs.append(pl.BlockSpec((1, tn), lambda i, j: (0, j)))
        operands.append(col_scale.reshape(1, n))
    return pl.pallas_call(
        functools.partial(_norm_inproj_kernel, scaled=int(col_scale is not None), n_cast=len(cast_weights)),
        grid=(steps, n // tn),
        in_specs=in_specs + [slab_block(wgt) for wgt in cast_weights],
        out_specs=[pl.BlockSpec((tm, tn), lambda i, j: (i, j))] + [slab_block(wgt) for wgt in cast_weights],
        out_shape=([jax.ShapeDtypeStruct((m, n), BF16)]
                   + [jax.ShapeDtypeStruct(wgt.shape, BF16) for wgt in cast_weights]),
        scratch_shapes=[pltpu.VMEM((tm, d), BF16)],
        compiler_params=pltpu.CompilerParams(
            dimension_semantics=("parallel", "arbitrary"), vmem_limit_bytes=VMEM_LIMIT_BYTES),
        name="norm_inproj",
    )(*operands, *cast_weights)


def _attn_kernel(q_ref, k_ref, v_ref, o_ref, acc_ref, run_ref, *, seq, blk, heads):
    dh = SB_HEAD_DIM
    row = lax.broadcasted_iota(jnp.int32, (blk, blk), 0)
    col = lax.broadcasted_iota(jnp.int32, (blk, blk), 1)
    later_keys = jnp.where(row > col, 1.0, 0.0).astype(BF16)
    causal = col < row

    def block(q0, k0, masked):
        group = heads // HEAD_GROUPS
        log_beta, stay_terms, later = {}, {}, {}
        for g in range(HEAD_GROUPS):
            group_heads = range(g * group, (g + 1) * group)
            for h in group_heads:
                lanes = slice(h * dh, (h + 1) * dh)
                q = q_ref[pl.ds(q0, blk), lanes]
                kb = k_ref[pl.ds(k0, blk), lanes]
                z = lax.dot_general(q, kb, (((1,), (1,)), ((), ())), preferred_element_type=F32)
                tail = jnp.log2(1.0 + jnp.exp2(-jnp.abs(z)))
                lb = jnp.minimum(z, 0.0) - tail
                sp = z - lb
                log_beta[h] = lb
                stay_terms[h] = jnp.where(causal, sp, 0.0) if masked else sp
            stacked = jnp.concatenate([stay_terms[h].astype(BF16) for h in group_heads], axis=0)
            sums = jnp.dot(stacked, later_keys, preferred_element_type=F32)
            for n, h in enumerate(group_heads):
                later[h] = sums[n * blk:(n + 1) * blk]
        for h in range(heads):
            lanes = slice(h * dh, (h + 1) * dh)
            run = run_ref[:, lanes]
            w = jnp.exp2(log_beta[h] - later[h])
            if masked:
                w = jnp.where(causal, w, 0.0)
            vb = v_ref[pl.ds(k0, blk), lanes]
            pv = jnp.dot(w.astype(BF16), vb, preferred_element_type=F32)
            acc_ref[:, lanes] += jnp.exp2(-run) * pv
            run_ref[:, lanes] = run + jnp.sum(stay_terms[h], axis=-1, keepdims=True)

    def q_body(qi, carry):
        q0 = pl.multiple_of(qi * blk, blk)
        acc_ref[...] = jnp.zeros_like(acc_ref)
        run_ref[...] = jnp.zeros_like(run_ref)
        block(q0, q0, True)

        def k_body(n, c):
            k0 = pl.multiple_of((qi - 1 - n) * blk, blk)
            block(q0, k0, False)
            return c

        lax.fori_loop(0, qi, k_body, 0)
        o_ref[pl.ds(q0, blk), :] = acc_ref[...].astype(o_ref.dtype)
        return carry

    lax.fori_loop(0, seq // blk, q_body, 0)


def _attention(p, batch, seq, heads):
    blk = ATTN_BLOCK
    assert seq % blk == 0 and SB_HEAD_DIM == LANES
    width = heads * SB_HEAD_DIM
    return pl.pallas_call(
        functools.partial(_attn_kernel, seq=seq, blk=blk, heads=heads),
        grid=(batch,),
        in_specs=[
            pl.BlockSpec((seq, width), lambda b: (b, 0)),
            pl.BlockSpec((seq, width), lambda b: (b, 1)),
            pl.BlockSpec((seq, width), lambda b: (b, 2)),
        ],
        out_specs=pl.BlockSpec((seq, width), lambda b: (b, 0)),
        out_shape=jax.ShapeDtypeStruct((batch * seq, width), BF16),
        scratch_shapes=[pltpu.VMEM((blk, width), F32), pltpu.VMEM((blk, width), F32)],
        compiler_params=pltpu.CompilerParams(
            dimension_semantics=("parallel",), vmem_limit_bytes=VMEM_LIMIT_BYTES),
        name="stickbreak_attention",
    )(p, p, p)


def _outproj_residual(y_ref, wo_ref, ln_ref, x_ref, o_ref, rows):
    o = jnp.dot(y_ref[rows, :], wo_ref[...], preferred_element_type=F32)
    ms = jnp.mean(o * o, axis=-1, keepdims=True)
    o_ref[rows, :] = x_ref[rows, :] + o * lax.rsqrt(ms + EPS) * ln_ref[...]


def _even_tail_kernel(a_ref, u_ref, uh_ref, g_ref, x_ref, pw_ref, ps_ref, wo_ref, ln_ref, *rest,
                      seq, tm, sb_width, group_dim, n_cast):
    cast_in, (o_ref, *cast_out), y_ref = rest[:n_cast], rest[n_cast:2 * n_cast + 1], rest[-1]
    for src, dst in zip(cast_in, cast_out):
        dst[...] = src[...].astype(BF16)

    i = pl.program_id(0)
    pos0 = (i * tm) % seq
    halo, chunk = POOL_HALO, POOL_CHUNK
    u_hist0 = jnp.where(pos0 == 0, jnp.zeros((halo, u_ref.shape[1]), BF16), uh_ref[...])

    lag = (lax.broadcasted_iota(jnp.int32, (chunk, chunk), 0)
           - lax.broadcasted_iota(jnp.int32, (chunk, chunk), 1))
    lag_hist = (lax.broadcasted_iota(jnp.int32, (halo, halo), 0) + halo
                - lax.broadcasted_iota(jnp.int32, (halo, halo), 1))

    bands = [jnp.where((lag >= 0) & (lag < win), 1.0, 0.0).astype(BF16) for win in POOL_WINDOWS]
    bands_hist = [jnp.where(lag_hist < win, 1.0, 0.0).astype(BF16) for win in POOL_WINDOWS]

    for r0 in range(0, tm, chunk):
        rows = slice(r0, r0 + chunk)
        y_ref[rows, :sb_width] = (a_ref[rows, :].astype(F32)
                                  * _silu(g_ref[rows, :sb_width].astype(F32))).astype(BF16)
        pos = pos0 + r0 + lax.broadcasted_iota(jnp.int32, (chunk, 1), 0)
        group_cols = [slice(gi * group_dim, (gi + 1) * group_dim) for gi in range(len(POOL_WINDOWS))]
        ugs = [u_ref[rows, cols] for cols in group_cols]
        wsums = []
        for gi, cols in enumerate(group_cols):
            hist = u_hist0[:, cols] if r0 == 0 else u_ref[r0 - halo:r0, cols]
            wsum = jnp.dot(bands[gi], ugs[gi], preferred_element_type=F32)
            head = wsum[:halo] + jnp.dot(bands_hist[gi], hist, preferred_element_type=F32)
            wsums.append(jnp.concatenate([head, wsum[halo:]], axis=0))
        pooled = []
        for gi, win in enumerate(POOL_WINDOWS):
            inv_count = 1.0 / jnp.minimum(win, pos + 1).astype(F32)
            pooled.append((wsums[gi] * inv_count - ugs[gi].astype(F32)).astype(BF16))
        ygs = [jnp.dot(pooled[gi], pw_ref[gi], preferred_element_type=F32) for gi in range(len(group_cols))]
        for gi, cols in enumerate(group_cols):
            out_cols = slice(sb_width + cols.start, sb_width + cols.stop)
            gate = _silu(g_ref[rows, out_cols].astype(F32))
            y_ref[rows, out_cols] = (ygs[gi] * ps_ref[:, cols] * gate).astype(BF16)
        _outproj_residual(y_ref, wo_ref, ln_ref, x_ref, o_ref, rows)


def _even_tail(a, p, x2d, pool_w, pool_scale, w_out, ln_post, seq, sb_width, cast_weights=()):
    m, d = x2d.shape
    tm, halo = EVEN_TM, POOL_HALO
    steps = m // tm
    slab = lambda wgt: (wgt.shape[0] // steps, wgt.shape[1])
    assert all(wgt.shape[0] % (steps * 16) == 0 for wgt in cast_weights)
    pool_width = d - sb_width
    groups, group_dim, _ = pool_w.shape
    assert seq % tm == 0 and tm % POOL_CHUNK == 0 and POOL_CHUNK % halo == 0 and groups == len(POOL_WINDOWS)
    assert halo >= max(POOL_WINDOWS) - 1 and pool_width == sb_width and d % pool_width == 0
    u_col = (3 * sb_width) // pool_width
    g_col = (3 * sb_width + pool_width) // d
    assert u_col * pool_width == 3 * sb_width and g_col * d == 3 * sb_width + pool_width
    const = lambda i: (0, 0)
    row_block = lambda i: (i, 0)
    return pl.pallas_call(
        functools.partial(_even_tail_kernel, seq=seq, tm=tm, sb_width=sb_width, group_dim=group_dim,
                          n_cast=len(cast_weights)),
        grid=(steps,),
        in_specs=[
            pl.BlockSpec((tm, sb_width), row_block),
            pl.BlockSpec((tm, pool_width), lambda i: (i, u_col)),
            pl.BlockSpec((halo, pool_width), lambda i: (jnp.maximum(i * (tm // halo) - 1, 0), u_col)),
            pl.BlockSpec((tm, d), lambda i: (i, g_col)),
            pl.BlockSpec((tm, d), row_block),
            _resident((groups, group_dim, group_dim), lambda i: (0, 0, 0)),
            _resident((1, pool_width), const),
            _resident((d, d), const),
            _resident((1, d), const),
        ] + [pl.BlockSpec(slab(wgt), row_block) for wgt in cast_weights],
        out_specs=[pl.BlockSpec((tm, d), row_block)] + [pl.BlockSpec(slab(wgt), row_block) for wgt in cast_weights],
        out_shape=([jax.ShapeDtypeStruct((m, d), F32)]
                   + [jax.ShapeDtypeStruct(wgt.shape, BF16) for wgt in cast_weights]),
        scratch_shapes=[pltpu.VMEM((tm, d), BF16)],
        compiler_params=pltpu.CompilerParams(
            dimension_semantics=("parallel",), vmem_limit_bytes=VMEM_LIMIT_BYTES),
        name="even_tail",
    )(a, p, p, p, x2d, pool_w, pool_scale.reshape(1, pool_width), w_out, ln_post.reshape(1, d), *cast_weights)


def _odd_tail_kernel(p_ref, ph_ref, x_ref, sw_ref, dw_ref, db_ref, cg_ref, cb_ref, wo_ref, ln_ref, o_ref,
                     zbuf, dbuf, cbuf, y_ref, *, seq, tm, width):
    i = pl.program_id(0)
    halo = CONV_HALO
    w = width
    planes = w // LANES
    keep = jnp.where((i * tm) % seq == 0, 0.0, 1.0).astype(F32)

    def piece(ref, n, c):
        return ref[:, n * w + c * LANES:n * w + (c + 1) * LANES].astype(F32)

    for c in range(planes):
        zbuf[c, :halo, :] = piece(ph_ref, 2, c) * piece(ph_ref, 0, c) * keep
        zbuf[c, halo:, :] = piece(p_ref, 2, c) * piece(p_ref, 0, c)
        dbuf[c, :halo, :] = piece(ph_ref, 3, c) * _sigmoid(piece(ph_ref, 4, c)) * keep
        dbuf[c, halo:, :] = piece(p_ref, 3, c) * _sigmoid(piece(p_ref, 4, c))

    for ci, r0 in enumerate(range(0, tm, CONV_ROWS)):
        rows = slice(r0, r0 + CONV_ROWS)
        slot = ci % cbuf.shape[0]
        for c in range(planes):
            lanes = slice(c * LANES, (c + 1) * LANES)
            c3 = jnp.zeros((CONV_ROWS, LANES), F32)
            for k in range(SCONV_K):
                off = halo - (SCONV_K - 1) + k + r0
                c3 = c3 + sw_ref[k:k + 1, lanes] * zbuf[c, off:off + CONV_ROWS, :]
            gate_c = _silu(p_ref[rows, 5 * w + c * LANES:5 * w + (c + 1) * LANES].astype(F32))
            bc = p_ref[rows, w + c * LANES:w + (c + 1) * LANES].astype(F32)
            y_ref[rows, lanes] = (bc * c3 * gate_c).astype(BF16)

            acc = jnp.broadcast_to(db_ref[:, lanes], (CONV_ROWS, LANES))
            for k in range(CONF_K):
                off = halo - (CONF_K - 1) + k + r0
                acc = acc + dw_ref[k:k + 1, lanes] * dbuf[c, off:off + CONV_ROWS, :]
            cbuf[slot, :, lanes] = acc

        d = cbuf[slot]
        mu = jnp.mean(d, axis=-1, keepdims=True)
        dc = d - mu
        var = jnp.mean(dc * dc, axis=-1, keepdims=True)
        dn = _silu(dc * lax.rsqrt(var + EPS) * cg_ref[...] + cb_ref[...])
        gate_d = _silu(p_ref[rows, 6 * w:7 * w].astype(F32))
        y_ref[rows, w:] = (dn * gate_d).astype(BF16)
        if (r0 + CONV_ROWS) % OUT_ROWS == 0:
            _outproj_residual(y_ref, wo_ref, ln_ref, x_ref, o_ref, slice(r0 + CONV_ROWS - OUT_ROWS, r0 + CONV_ROWS))


def _odd_tail(p, x2d, sconv_w, dconv_w, dconv_b, cnorm_g, cnorm_b, w_out, ln_post, seq, width):
    m, d = x2d.shape
    n_in = p.shape[1]
    tm, halo = ODD_TM, CONV_HALO
    assert seq % tm == 0 and tm % halo == 0 and tm % OUT_ROWS == 0 and OUT_ROWS % CONV_ROWS == 0
    assert halo >= CONF_K - 1
    assert d == 2 * width and n_in == 7 * width and width % LANES == 0
    assert sconv_w.shape == (SCONV_K, width) and dconv_w.shape == (CONF_K, width)
    const = lambda i: (0, 0)
    return pl.pallas_call(
        functools.partial(_odd_tail_kernel, seq=seq, tm=tm, width=width),
        grid=(m // tm,),
        in_specs=[
            pl.BlockSpec((tm, n_in), lambda i: (i, 0)),
            pl.BlockSpec((halo, n_in), lambda i: (jnp.maximum(i * (tm // halo) - 1, 0), 0)),
            pl.BlockSpec((tm, d), lambda i: (i, 0)),
            _resident((SCONV_K, width), const),
            _resident((CONF_K, width), const),
            _resident((1, width), const),
            _resident((1, width), const),
            _resident((1, width), const),
            _resident((d, d), const),
            _resident((1, d), const),
        ],
        out_specs=pl.BlockSpec((tm, d), lambda i: (i, 0)),
        out_shape=jax.ShapeDtypeStruct((m, d), F32),
        scratch_shapes=[
            pltpu.VMEM((width // LANES, halo + tm, LANES), F32),
            pltpu.VMEM((width // LANES, halo + tm, LANES), F32),
            pltpu.VMEM((2, CONV_ROWS, width), F32),
            pltpu.VMEM((tm, d), BF16),
        ],
        compiler_params=pltpu.CompilerParams(
            dimension_semantics=("parallel",), vmem_limit_bytes=VMEM_LIMIT_BYTES),
        name="odd_tail",
    )(p, p, x2d, sconv_w, dconv_w, dconv_b.reshape(1, width), cnorm_g.reshape(1, width),
      cnorm_b.reshape(1, width), w_out, ln_post.reshape(1, d))


def kernel(x, ln_pre_even, w_in_even, pool_w, pool_scale, w_out_even, ln_post_even,
           ln_pre_odd, w_in_odd, sconv_w, dconv_w, dconv_b, cnorm_g, cnorm_b, w_out_odd, ln_post_odd):
    batch, seq, d = x.shape
    depth = ln_pre_even.shape[0] + ln_pre_odd.shape[0]
    sb_width = d // 2
    heads = sb_width // SB_HEAD_DIM
    conv_width = d // 2
    q_scale = math.log2(math.e) / math.sqrt(SB_HEAD_DIM)

    xf = x.reshape(batch * seq, d)
    odd_weights = None
    for layer in range(depth):
        i = layer // 2
        if layer % 2 == 0:
            n_in = w_in_even.shape[-1]
            col_scale = jnp.where(jnp.arange(n_in) < sb_width, q_scale, 1.0).astype(F32)
            p, w_out, pool_mat = _norm_inproj(
                xf, ln_pre_even[i], w_in_even[i].astype(BF16), col_scale,
                (w_out_even[i], pool_w[i].reshape(-1, pool_w.shape[-1])))
            a = _attention(p, batch, seq, heads)
            coming = (w_in_odd[i], w_out_odd[i]) if layer + 1 < depth else ()
            xf, *rounded = _even_tail(a, p, xf, pool_mat.reshape(pool_w.shape[1:]), pool_scale[i], w_out,
                                      ln_post_even[i], seq, sb_width, coming)
            odd_weights = rounded or None
        else:
            w_in, w_out = odd_weights or (w_in_odd[i].astype(BF16), w_out_odd[i].astype(BF16))
            p, = _norm_inproj(xf, ln_pre_odd[i], w_in)
            xf = _odd_tail(p, xf, sconv_w[i], dconv_w[i], dconv_b[i], cnorm_g[i], cnorm_b[i],
                           w_out, ln_post_odd[i], seq, conv_width)
    return xf.reshape(batch, seq, d)
```

```python
import functools
import math

import jax
import jax.numpy as jnp
from jax import lax
from jax.experimental import pallas as pl
from jax.experimental.pallas import tpu as pltpu

F32 = jnp.float32
BF16 = jnp.bfloat16
EPS = 1e-6

SB_HEAD_DIM = 128
POOL_WINDOWS = (2, 4, 8, 16)
SCONV_K = 3
CONF_K = 31

VMEM_LIMIT_BYTES = 56 * 1024 * 1024
LANES = 128
MXU_DIM = 256

INPROJ_TM = 1024
INPROJ_TN_MAX = 2048
NORM_ROWS = 256
ATTN_BLOCK = 256
HEAD_GROUPS = 2
EVEN_TM = 512
ODD_TM = 512
POOL_CHUNK = 256
POOL_HALO = 16
CONV_HALO = 32
CONV_ROWS = 128
OUT_ROWS = 256


def _sigmoid(v):
    return 1.0 / (1.0 + jnp.exp(-v))


def _silu(v):
    return v * _sigmoid(v)


def _resident(block_shape, index_map):
    return pl.BlockSpec(block_shape, index_map, pipeline_mode=pl.Buffered(1))


def _norm_inproj_kernel(x_ref, g_ref, w_ref, *rest, scaled, n_cast):
    h_ref = rest[-1]
    maybe_scale_ref, rest = rest[:scaled], rest[scaled:-1]
    cast_in, (o_ref, *cast_out) = rest[:n_cast], rest[n_cast:]

    def project(rows, h):
        o = jnp.dot(h, w_ref[...], preferred_element_type=F32)
        if maybe_scale_ref:
            o = o * maybe_scale_ref[0][...]
        o_ref[rows, :] = o.astype(o_ref.dtype)

    @pl.when(pl.program_id(1) == 0)
    def _():
        for src, dst in zip(cast_in, cast_out):
            dst[...] = src[...].astype(BF16)
        for r0 in range(0, x_ref.shape[0], NORM_ROWS):
            rows = slice(r0, r0 + NORM_ROWS)
            x = x_ref[rows, :]
            ms = jnp.mean(x * x, axis=-1, keepdims=True)
            h = (x * lax.rsqrt(ms + EPS) * g_ref[...]).astype(BF16)
            h_ref[rows, :] = h
            project(rows, h)

    @pl.when(pl.program_id(1) > 0)
    def _():
        project(slice(None), h_ref[...])


def _norm_inproj(x2d, gain, w, col_scale=None, cast_weights=()):
    m, d = x2d.shape
    n = w.shape[1]
    tm = INPROJ_TM
    tn = max(t for t in range(MXU_DIM, INPROJ_TN_MAX + 1, MXU_DIM) if n % t == 0)
    assert m % tm == 0
    steps = m // tm
    slab = lambda wgt: (wgt.shape[0] // steps,) + wgt.shape[1:]
    slab_block = lambda wgt: pl.BlockSpec(slab(wgt), lambda i, j: (i,) + (0,) * (wgt.ndim - 1))
    assert all(wgt.shape[0] % steps == 0 and (slab(wgt)[-2] % 16 == 0) for wgt in cast_weights)
    in_specs = [
        pl.BlockSpec((tm, d), lambda i, j: (i, 0)),
        _resident((1, d), lambda i, j: (0, 0)),
        pl.BlockSpec((d, tn), lambda i, j: (0, j)),
    ]
    operands = [x2d, gain.reshape(1, d), w]
    if col_scale is not None:
        in_specs.append(pl.BlockSpec((1, tn), lambda i, j: (0, j)))
        operands.append(col_scale.reshape(1, n))
    return pl.pallas_call(
        functools.partial(_norm_inproj_kernel, scaled=int(col_scale is not None), n_cast=len(cast_weights)),
        grid=(steps, n // tn),
        in_specs=in_specs + [slab_block(wgt) for wgt in cast_weights],
        out_specs=[pl.BlockSpec((tm, tn), lambda i, j: (i, j))] + [slab_block(wgt) for wgt in cast_weights],
        out_shape=([jax.ShapeDtypeStruct((m, n), BF16)]
                   + [jax.ShapeDtypeStruct(wgt.shape, BF16) for wgt in cast_weights]),
        scratch_shapes=[pltpu.VMEM((tm, d), BF16)],
        compiler_params=pltpu.CompilerParams(
            dimension_semantics=("parallel", "arbitrary"), vmem_limit_bytes=VMEM_LIMIT_BYTES),
        name="norm_inproj",
    )(*operands, *cast_weights)


def _attn_kernel(q_ref, k_ref, v_ref, o_ref, acc_ref, run_ref, *, seq, blk, heads):
    dh = SB_HEAD_DIM
    row = lax.broadcasted_iota(jnp.int32, (blk, blk), 0)
    col = lax.broadcasted_iota(jnp.int32, (blk, blk), 1)
    later_keys = jnp.where(row > col, 1.0, 0.0).astype(BF16)
    causal = col < row

    def block(q0, k0, masked):
        group = heads // HEAD_GROUPS
        log_beta, stay_terms, later = {}, {}, {}
        for g in range(HEAD_GROUPS):
            group_heads = range(g * group, (g + 1) * group)
            for h in group_heads:
                lanes = slice(h * dh, (h + 1) * dh)
                q = q_ref[pl.ds(q0, blk), lanes]
                kb = k_ref[pl.ds(k0, blk), lanes]
                z = lax.dot_general(q, kb, (((1,), (1,)), ((), ())), preferred_element_type=F32)
                tail = jnp.log2(1.0 + jnp.exp2(-jnp.abs(z)))
                lb = jnp.minimum(z, 0.0) - tail
                sp = z - lb
                log_beta[h] = lb
                stay_terms[h] = jnp.where(causal, sp, 0.0) if masked else sp
            stacked = jnp.concatenate([stay_terms[h].astype(BF16) for h in group_heads], axis=0)
            sums = jnp.dot(stacked, later_keys, preferred_element_type=F32)
            for n, h in enumerate(group_heads):
                later[h] = sums[n * blk:(n + 1) * blk]
        for h in range(heads):
            lanes = slice(h * dh, (h + 1) * dh)
            run = run_ref[:, lanes]
            w = jnp.exp2(log_beta[h] - later[h])
            if masked:
                w = jnp.where(causal, w, 0.0)
            vb = v_ref[pl.ds(k0, blk), lanes]
            pv = jnp.dot(w.astype(BF16), vb, preferred_element_type=F32)
            acc_ref[:, lanes] += jnp.exp2(-run) * pv
            run_ref[:, lanes] = run + jnp.sum(stay_terms[h], axis=-1, keepdims=True)

    def q_body(qi, carry):
        q0 = pl.multiple_of(qi * blk, blk)
        acc_ref[...] = jnp.zeros_like(acc_ref)
        run_ref[...] = jnp.zeros_like(run_ref)
        block(q0, q0, True)

        def k_body(n, c):
            k0 = pl.multiple_of((qi - 1 - n) * blk, blk)
            block(q0, k0, False)
            return c

        lax.fori_loop(0, qi, k_body, 0)
        o_ref[pl.ds(q0, blk), :] = acc_ref[...].astype(o_ref.dtype)
        return carry

    lax.fori_loop(0, seq // blk, q_body, 0)


def _attention(p, batch, seq, heads):
    blk = ATTN_BLOCK
    assert seq % blk == 0 and SB_HEAD_DIM == LANES
    width = heads * SB_HEAD_DIM
    return pl.pallas_call(
        functools.partial(_attn_kernel, seq=seq, blk=blk, heads=heads),
        grid=(batch,),
        in_specs=[
            pl.BlockSpec((seq, width), lambda b: (b, 0)),
            pl.BlockSpec((seq, width), lambda b: (b, 1)),
            pl.BlockSpec((seq, width), lambda b: (b, 2)),
        ],
        out_specs=pl.BlockSpec((seq, width), lambda b: (b, 0)),
        out_shape=jax.ShapeDtypeStruct((batch * seq, width), BF16),
        scratch_shapes=[pltpu.VMEM((blk, width), F32), pltpu.VMEM((blk, width), F32)],
        compiler_params=pltpu.CompilerParams(
            dimension_semantics=("parallel",), vmem_limit_bytes=VMEM_LIMIT_BYTES),
        name="stickbreak_attention",
    )(p, p, p)


def _outproj_residual(y_ref, wo_ref, ln_ref, x_ref, o_ref, rows):
    o = jnp.dot(y_ref[rows, :], wo_ref[...], preferred_element_type=F32)
    ms = jnp.mean(o * o, axis=-1, keepdims=True)
    o_ref[rows, :] = x_ref[rows, :] + o * lax.rsqrt(ms + EPS) * ln_ref[...]


def _even_tail_kernel(a_ref, u_ref, uh_ref, g_ref, x_ref, pw_ref, ps_ref, wo_ref, ln_ref, *rest,
                      seq, tm, sb_width, group_dim, n_cast):
    cast_in, (o_ref, *cast_out), y_ref = rest[:n_cast], rest[n_cast:2 * n_cast + 1], rest[-1]
    for src, dst in zip(cast_in, cast_out):
        dst[...] = src[...].astype(BF16)

    i = pl.program_id(0)
    pos0 = (i * tm) % seq
    halo, chunk = POOL_HALO, POOL_CHUNK
    u_hist0 = jnp.where(pos0 == 0, jnp.zeros((halo, u_ref.shape[1]), BF16), uh_ref[...])

    lag = (lax.broadcasted_iota(jnp.int32, (chunk, chunk), 0)
           - lax.broadcasted_iota(jnp.int32, (chunk, chunk), 1))
    lag_hist = (lax.broadcasted_iota(jnp.int32, (halo, halo), 0) + halo
                - lax.broadcasted_iota(jnp.int32, (halo, halo), 1))

    bands = [jnp.where((lag >= 0) & (lag < win), 1.0, 0.0).astype(BF16) for win in POOL_WINDOWS]
    bands_hist = [jnp.where(lag_hist < win, 1.0, 0.0).astype(BF16) for win in POOL_WINDOWS]

    for r0 in range(0, tm, chunk):
        rows = slice(r0, r0 + chunk)
        y_ref[rows, :sb_width] = (a_ref[rows, :].astype(F32)
                                  * _silu(g_ref[rows, :sb_width].astype(F32))).astype(BF16)
        pos = pos0 + r0 + lax.broadcasted_iota(jnp.int32, (chunk, 1), 0)
        group_cols = [slice(gi * group_dim, (gi + 1) * group_dim) for gi in range(len(POOL_WINDOWS))]
        ugs = [u_ref[rows, cols] for cols in group_cols]
        wsums = []
        for gi, cols in enumerate(group_cols):
            hist = u_hist0[:, cols] if r0 == 0 else u_ref[r0 - halo:r0, cols]
            wsum = jnp.dot(bands[gi], ugs[gi], preferred_element_type=F32)
            head = wsum[:halo] + jnp.dot(bands_hist[gi], hist, preferred_element_type=F32)
            wsums.append(jnp.concatenate([head, wsum[halo:]], axis=0))
        pooled = []
        for gi, win in enumerate(POOL_WINDOWS):
            inv_count = 1.0 / jnp.minimum(win, pos + 1).astype(F32)
            pooled.append((wsums[gi] * inv_count - ugs[gi].astype(F32)).astype(BF16))
        ygs = [jnp.dot(pooled[gi], pw_ref[gi], preferred_element_type=F32) for gi in range(len(group_cols))]
        for gi, cols in enumerate(group_cols):
            out_cols = slice(sb_width + cols.start, sb_width + cols.stop)
            gate = _silu(g_ref[rows, out_cols].astype(F32))
            y_ref[rows, out_cols] = (ygs[gi] * ps_ref[:, cols] * gate).astype(BF16)
        _outproj_residual(y_ref, wo_ref, ln_ref, x_ref, o_ref, rows)


def _even_tail(a, p, x2d, pool_w, pool_scale, w_out, ln_post, seq, sb_width, cast_weights=()):
    m, d = x2d.shape
    tm, halo = EVEN_TM, POOL_HALO
    steps = m // tm
    slab = lambda wgt: (wgt.shape[0] // steps, wgt.shape[1])
    assert all(wgt.shape[0] % (steps * 16) == 0 for wgt in cast_weights)
    pool_width = d - sb_width
    groups, group_dim, _ = pool_w.shape
    assert seq % tm == 0 and tm % POOL_CHUNK == 0 and POOL_CHUNK % halo == 0 and groups == len(POOL_WINDOWS)
    assert halo >= max(POOL_WINDOWS) - 1 and pool_width == sb_width and d % pool_width == 0
    u_col = (3 * sb_width) // pool_width
    g_col = (3 * sb_width + pool_width) // d
    assert u_col * pool_width == 3 * sb_width and g_col * d == 3 * sb_width + pool_width
    const = lambda i: (0, 0)
    row_block = lambda i: (i, 0)
    return pl.pallas_call(
        functools.partial(_even_tail_kernel, seq=seq, tm=tm, sb_width=sb_width, group_dim=group_dim,
                          n_cast=len(cast_weights)),
        grid=(steps,),
        in_specs=[
            pl.BlockSpec((tm, sb_width), row_block),
            pl.BlockSpec((tm, pool_width), lambda i: (i, u_col)),
            pl.BlockSpec((halo, pool_width), lambda i: (jnp.maximum(i * (tm // halo) - 1, 0), u_col)),
            pl.BlockSpec((tm, d), lambda i: (i, g_col)),
            pl.BlockSpec((tm, d), row_block),
            _resident((groups, group_dim, group_dim), lambda i: (0, 0, 0)),
            _resident((1, pool_width), const),
            _resident((d, d), const),
            _resident((1, d), const),
        ] + [pl.BlockSpec(slab(wgt), row_block) for wgt in cast_weights],
        out_specs=[pl.BlockSpec((tm, d), row_block)] + [pl.BlockSpec(slab(wgt), row_block) for wgt in cast_weights],
        out_shape=([jax.ShapeDtypeStruct((m, d), F32)]
                   + [jax.ShapeDtypeStruct(wgt.shape, BF16) for wgt in cast_weights]),
        scratch_shapes=[pltpu.VMEM((tm, d), BF16)],
        compiler_params=pltpu.CompilerParams(
            dimension_semantics=("parallel",), vmem_limit_bytes=VMEM_LIMIT_BYTES),
        name="even_tail",
    )(a, p, p, p, x2d, pool_w, pool_scale.reshape(1, pool_width), w_out, ln_post.reshape(1, d), *cast_weights)


def _odd_tail_kernel(p_ref, ph_ref, x_ref, sw_ref, dw_ref, db_ref, cg_ref, cb_ref, wo_ref, ln_ref, o_ref,
                     zbuf, dbuf, cbuf, y_ref, *, seq, tm, width):
    i = pl.program_id(0)
    halo = CONV_HALO
    w = width
    planes = w // LANES
    keep = jnp.where((i * tm) % seq == 0, 0.0, 1.0).astype(F32)

    def piece(ref, n, c):
        return ref[:, n * w + c * LANES:n * w + (c + 1) * LANES].astype(F32)

    for c in range(planes):
        zbuf[c, :halo, :] = piece(ph_ref, 2, c) * piece(ph_ref, 0, c) * keep
        zbuf[c, halo:, :] = piece(p_ref, 2, c) * piece(p_ref, 0, c)
        dbuf[c, :halo, :] = piece(ph_ref, 3, c) * _sigmoid(piece(ph_ref, 4, c)) * keep
        dbuf[c, halo:, :] = piece(p_ref, 3, c) * _sigmoid(piece(p_ref, 4, c))

    for ci, r0 in enumerate(range(0, tm, CONV_ROWS)):
        rows = slice(r0, r0 + CONV_ROWS)
        slot = ci % cbuf.shape[0]
        for c in range(planes):
            lanes = slice(c * LANES, (c + 1) * LANES)
            c3 = jnp.zeros((CONV_ROWS, LANES), F32)
            for k in range(SCONV_K):
                off = halo - (SCONV_K - 1) + k + r0
                c3 = c3 + sw_ref[k:k + 1, lanes] * zbuf[c, off:off + CONV_ROWS, :]
            gate_c = _silu(p_ref[rows, 5 * w + c * LANES:5 * w + (c + 1) * LANES].astype(F32))
            bc = p_ref[rows, w + c * LANES:w + (c + 1) * LANES].astype(F32)
            y_ref[rows, lanes] = (bc * c3 * gate_c).astype(BF16)

            acc = jnp.broadcast_to(db_ref[:, lanes], (CONV_ROWS, LANES))
            for k in range(CONF_K):
                off = halo - (CONF_K - 1) + k + r0
                acc = acc + dw_ref[k:k + 1, lanes] * dbuf[c, off:off + CONV_ROWS, :]
            cbuf[slot, :, lanes] = acc

        d = cbuf[slot]
        mu = jnp.mean(d, axis=-1, keepdims=True)
        dc = d - mu
        var = jnp.mean(dc * dc, axis=-1, keepdims=True)
        dn = _silu(dc * lax.rsqrt(var + EPS) * cg_ref[...] + cb_ref[...])
        gate_d = _silu(p_ref[rows, 6 * w:7 * w].astype(F32))
        y_ref[rows, w:] = (dn * gate_d).astype(BF16)
        if (r0 + CONV_ROWS) % OUT_ROWS == 0:
            _outproj_residual(y_ref, wo_ref, ln_ref, x_ref, o_ref, slice(r0 + CONV_ROWS - OUT_ROWS, r0 + CONV_ROWS))


def _odd_tail(p, x2d, sconv_w, dconv_w, dconv_b, cnorm_g, cnorm_b, w_out, ln_post, seq, width):
    m, d = x2d.shape
    n_in = p.shape[1]
    tm, halo = ODD_TM, CONV_HALO
    assert seq % tm == 0 and tm % halo == 0 and tm % OUT_ROWS == 0 and OUT_ROWS % CONV_ROWS == 0
    assert halo >= CONF_K - 1
    assert d == 2 * width and n_in == 7 * width and width % LANES == 0
    assert sconv_w.shape == (SCONV_K, width) and dconv_w.shape == (CONF_K, width)
    const = lambda i: (0, 0)
    return pl.pallas_call(
        functools.partial(_odd_tail_kernel, seq=seq, tm=tm, width=width),
        grid=(m // tm,),
        in_specs=[
            pl.BlockSpec((tm, n_in), lambda i: (i, 0)),
            pl.BlockSpec((halo, n_in), lambda i: (jnp.maximum(i * (tm // halo) - 1, 0), 0)),
            pl.BlockSpec((tm, d), lambda i: (i, 0)),
            _resident((SCONV_K, width), const),
            _resident((CONF_K, width), const),
            _resident((1, width), const),
            _resident((1, width), const),
            _resident((1, width), const),
            _resident((d, d), const),
            _resident((1, d), const),
        ],
        out_specs=pl.BlockSpec((tm, d), lambda i: (i, 0)),
        out_shape=jax.ShapeDtypeStruct((m, d), F32),
        scratch_shapes=[
            pltpu.VMEM((width // LANES, halo + tm, LANES), F32),
            pltpu.VMEM((width // LANES, halo + tm, LANES), F32),
            pltpu.VMEM((2, CONV_ROWS, width), F32),
            pltpu.VMEM((tm, d), BF16),
        ],
        compiler_params=pltpu.CompilerParams(
            dimension_semantics=("parallel",), vmem_limit_bytes=VMEM_LIMIT_BYTES),
        name="odd_tail",
    )(p, p, x2d, sconv_w, dconv_w, dconv_b.reshape(1, width), cnorm_g.reshape(1, width),
      cnorm_b.reshape(1, width), w_out, ln_post.reshape(1, d))


def kernel(x, ln_pre_even, w_in_even, pool_w, pool_scale, w_out_even, ln_post_even,
           ln_pre_odd, w_in_odd, sconv_w, dconv_w, dconv_b, cnorm_g, cnorm_b, w_out_odd, ln_post_odd):
    batch, seq, d = x.shape
    depth = ln_pre_even.shape[0] + ln_pre_odd.shape[0]
    sb_width = d // 2
    heads = sb_width // SB_HEAD_DIM
    conv_width = d // 2
    q_scale = math.log2(math.e) / math.sqrt(SB_HEAD_DIM)

    xf = x.reshape(batch * seq, d)
    odd_weights = None
    for layer in range(depth):
        i = layer // 2
        if layer % 2 == 0:
            n_in = w_in_even.shape[-1]
            col_scale = jnp.where(jnp.arange(n_in) < sb_width, q_scale, 1.0).astype(F32)
            p, w_out, pool_mat = _norm_inproj(
                xf, ln_pre_even[i], w_in_even[i].astype(BF16), col_scale,
                (w_out_even[i], pool_w[i].reshape(-1, pool_w.shape[-1])))
            a = _attention(p, batch, seq, heads)
            coming = (w_in_odd[i], w_out_odd[i]) if layer + 1 < depth else ()
            xf, *rounded = _even_tail(a, p, xf, pool_mat.reshape(pool_w.shape[1:]), pool_scale[i], w_out,
                                      ln_post_even[i], seq, sb_width, coming)
            odd_weights = rounded or None
        else:
            w_in, w_out = odd_weights or (w_in_odd[i].astype(BF16), w_out_odd[i].astype(BF16))
            p, = _norm_inproj(xf, ln_pre_odd[i], w_in)
            xf = _odd_tail(p, xf, sconv_w[i], dconv_w[i], dconv_b[i], cnorm_g[i], cnorm_b[i],
                           w_out, ln_post_odd[i], seq, conv_width)
    return xf.reshape(batch, seq, d)
```

```python
import functools
import math

import jax
import jax.numpy as jnp
from jax import lax
from jax.experimental import pallas as pl
from jax.experimental.pallas import tpu as pltpu

F32 = jnp.float32
BF16 = jnp.bfloat16
EPS = 1e-6

SB_HEAD_DIM = 128
POOL_WINDOWS = (2, 4, 8, 16)
SCONV_K = 3
CONF_K = 31

VMEM_LIMIT_BYTES = 56 * 1024 * 1024
LANES = 128
MXU_DIM = 256

INPROJ_TM = 1024
INPROJ_TN_MAX = 2048
NORM_ROWS = 256
ATTN_BLOCK = 256
HEAD_GROUPS = 2
EVEN_TM = 512
ODD_TM = 512
POOL_CHUNK = 256
POOL_HALO = 16
CONV_HALO = 32
CONV_ROWS = 128
OUT_ROWS = 256


def _sigmoid(v):
    return 1.0 / (1.0 + jnp.exp(-v))


def _silu(v):
    return v * _sigmoid(v)


def _resident(block_shape, index_map):
    return pl.BlockSpec(block_shape, index_map, pipeline_mode=pl.Buffered(1))


def _norm_inproj_kernel(x_ref, g_ref, w_ref, *rest, scaled, n_cast):
    h_ref = rest[-1]
    maybe_scale_ref, rest = rest[:scaled], rest[scaled:-1]
    cast_in, (o_ref, *cast_out) = rest[:n_cast], rest[n_cast:]

    def project(rows, h):
        o = jnp.dot(h, w_ref[...], preferred_element_type=F32)
        if maybe_scale_ref:
            o = o * maybe_scale_ref[0][...]
        o_ref[rows, :] = o.astype(o_ref.dtype)

    @pl.when(pl.program_id(1) == 0)
    def _():
        for src, dst in zip(cast_in, cast_out):
            dst[...] = src[...].astype(BF16)
        for r0 in range(0, x_ref.shape[0], NORM_ROWS):
            rows = slice(r0, r0 + NORM_ROWS)
            x = x_ref[rows, :]
            ms = jnp.mean(x * x, axis=-1, keepdims=True)
            h = (x * lax.rsqrt(ms + EPS) * g_ref[...]).astype(BF16)
            h_ref[rows, :] = h
            project(rows, h)

    @pl.when(pl.program_id(1) > 0)
    def _():
        project(slice(None), h_ref[...])


def _norm_inproj(x2d, gain, w, col_scale=None, cast_weights=()):
    m, d = x2d.shape
    n = w.shape[1]
    tm = INPROJ_TM
    tn = max(t for t in range(MXU_DIM, INPROJ_TN_MAX + 1, MXU_DIM) if n % t == 0)
    assert m % tm == 0
    steps = m // tm
    slab = lambda wgt: (wgt.shape[0] // steps,) + wgt.shape[1:]
    slab_block = lambda wgt: pl.BlockSpec(slab(wgt), lambda i, j: (i,) + (0,) * (wgt.ndim - 1))
    assert all(wgt.shape[0] % steps == 0 and (slab(wgt)[-2] % 16 == 0) for wgt in cast_weights)
    in_specs = [
        pl.BlockSpec((tm, d), lambda i, j: (i, 0)),
        _resident((1, d), lambda i, j: (0, 0)),
        pl.BlockSpec((d, tn), lambda i, j: (0, j)),
    ]
    operands = [x2d, gain.reshape(1, d), w]
    if col_scale is not None:
        in_specs.append(pl.BlockSpec((1, tn), lambda i, j: (0, j)))
        operands.append(col_scale.reshape(1, n))
    return pl.pallas_call(
        functools.partial(_norm_inproj_kernel, scaled=int(col_scale is not None), n_cast=len(cast_weights)),
        grid=(steps, n // tn),
        in_specs=in_specs + [slab_block(wgt) for wgt in cast_weights],
        out_specs=[pl.BlockSpec((tm, tn), lambda i, j: (i, j))] + [slab_block(wgt) for wgt in cast_weights],
        out_shape=([jax.ShapeDtypeStruct((m, n), BF16)]
                   + [jax.ShapeDtypeStruct(wgt.shape, BF16) for wgt in cast_weights]),
        scratch_shapes=[pltpu.VMEM((tm, d), BF16)],
        compiler_params=pltpu.CompilerParams(
            dimension_semantics=("parallel", "arbitrary"), vmem_limit_bytes=VMEM_LIMIT_BYTES),
        name="norm_inproj",
    )(*operands, *cast_weights)


def _attn_kernel(q_ref, k_ref, v_ref, o_ref, acc_ref, run_ref, *, seq, blk, heads):
    dh = SB_HEAD_DIM
    row = lax.broadcasted_iota(jnp.int32, (blk, blk), 0)
    col = lax.broadcasted_iota(jnp.int32, (blk, blk), 1)
    later_keys = jnp.where(row > col, 1.0, 0.0).astype(BF16)
    causal = col < row

    def block(q0, k0, masked):
        group = heads // HEAD_GROUPS
        log_beta, stay_terms, later = {}, {}, {}
        for g in range(HEAD_GROUPS):
            group_heads = range(g * group, (g + 1) * group)
            for h in group_heads:
                lanes = slice(h * dh, (h + 1) * dh)
                q = q_ref[pl.ds(q0, blk), lanes]
                kb = k_ref[pl.ds(k0, blk), lanes]
                z = lax.dot_general(q, kb, (((1,), (1,)), ((), ())), preferred_element_type=F32)
                tail = jnp.log2(1.0 + jnp.exp2(-jnp.abs(z)))
                lb = jnp.minimum(z, 0.0) - tail
                sp = z - lb
                log_beta[h] = lb
                stay_terms[h] = jnp.where(causal, sp, 0.0) if masked else sp
            stacked = jnp.concatenate([stay_terms[h].astype(BF16) for h in group_heads], axis=0)
            sums = jnp.dot(stacked, later_keys, preferred_element_type=F32)
            for n, h in enumerate(group_heads):
                later[h] = sums[n * blk:(n + 1) * blk]
        for h in range(heads):
            lanes = slice(h * dh, (h + 1) * dh)
            run = run_ref[:, lanes]
            w = jnp.exp2(log_beta[h] - later[h])
            if masked:
                w = jnp.where(causal, w, 0.0)
            vb = v_ref[pl.ds(k0, blk), lanes]
            pv = jnp.dot(w.astype(BF16), vb, preferred_element_type=F32)
            acc_ref[:, lanes] += jnp.exp2(-run) * pv
            run_ref[:, lanes] = run + jnp.sum(stay_terms[h], axis=-1, keepdims=True)

    def q_body(qi, carry):
        q0 = pl.multiple_of(qi * blk, blk)
        acc_ref[...] = jnp.zeros_like(acc_ref)
        run_ref[...] = jnp.zeros_like(run_ref)
        block(q0, q0, True)

        def k_body(n, c):
            k0 = pl.multiple_of((qi - 1 - n) * blk, blk)
            block(q0, k0, False)
            return c

        lax.fori_loop(0, qi, k_body, 0)
        o_ref[pl.ds(q0, blk), :] = acc_ref[...].astype(o_ref.dtype)
        return carry

    lax.fori_loop(0, seq // blk, q_body, 0)


def _attention(p, batch, seq, heads):
    blk = ATTN_BLOCK
    assert seq % blk == 0 and SB_HEAD_DIM == LANES
    width = heads * SB_HEAD_DIM
    return pl.pallas_call(
        functools.partial(_attn_kernel, seq=seq, blk=blk, heads=heads),
        grid=(batch,),
        in_specs=[
            pl.BlockSpec((seq, width), lambda b: (b, 0)),
            pl.BlockSpec((seq, width), lambda b: (b, 1)),
            pl.BlockSpec((seq, width), lambda b: (b, 2)),
        ],
        out_specs=pl.BlockSpec((seq, width), lambda b: (b, 0)),
        out_shape=jax.ShapeDtypeStruct((batch * seq, width), BF16),
        scratch_shapes=[pltpu.VMEM((blk, width), F32), pltpu.VMEM((blk, width), F32)],
        compiler_params=pltpu.CompilerParams(
            dimension_semantics=("parallel",), vmem_limit_bytes=VMEM_LIMIT_BYTES),
        name="stickbreak_attention",
    )(p, p, p)


def _outproj_residual(y_ref, wo_ref, ln_ref, x_ref, o_ref, rows):
    o = jnp.dot(y_ref[rows, :], wo_ref[...], preferred_element_type=F32)
    ms = jnp.mean(o * o, axis=-1, keepdims=True)
    o_ref[rows, :] = x_ref[rows, :] + o * lax.rsqrt(ms + EPS) * ln_ref[...]


def _even_tail_kernel(a_ref, u_ref, uh_ref, g_ref, x_ref, pw_ref, ps_ref, wo_ref, ln_ref, *rest,
                      seq, tm, sb_width, group_dim, n_cast):
    cast_in, (o_ref, *cast_out), (y_ref, pool_ref) = rest[:n_cast], rest[n_cast:2 * n_cast + 1], rest[-2:]
    for src, dst in zip(cast_in, cast_out):
        dst[...] = src[...].astype(BF16)

    i = pl.program_id(0)
    pos0 = (i * tm) % seq
    halo, chunk = POOL_HALO, POOL_CHUNK
    u_hist0 = jnp.where(pos0 == 0, jnp.zeros((halo, u_ref.shape[1]), BF16), uh_ref[...])

    lag = (lax.broadcasted_iota(jnp.int32, (chunk, chunk), 0)
           - lax.broadcasted_iota(jnp.int32, (chunk, chunk), 1))
    lag_hist = (lax.broadcasted_iota(jnp.int32, (halo, halo), 0) + halo
                - lax.broadcasted_iota(jnp.int32, (halo, halo), 1))

    bands = [jnp.where((lag >= 0) & (lag < win), 1.0, 0.0).astype(BF16) for win in POOL_WINDOWS]
    bands_hist = [jnp.where(lag_hist < win, 1.0, 0.0).astype(BF16) for win in POOL_WINDOWS]

    group_cols = [slice(gi * group_dim, (gi + 1) * group_dim) for gi in range(len(POOL_WINDOWS))]
    chunks = [slice(r0, r0 + chunk) for r0 in range(0, tm, chunk)]

    for rows in chunks:
        r0 = rows.start
        pos = pos0 + r0 + lax.broadcasted_iota(jnp.int32, (chunk, 1), 0)
        ugs = [u_ref[rows, cols] for cols in group_cols]
        wsums = []
        for gi, cols in enumerate(group_cols):
            hist = u_hist0[:, cols] if r0 == 0 else u_ref[r0 - halo:r0, cols]
            wsum = jnp.dot(bands[gi], ugs[gi], preferred_element_type=F32)
            head = wsum[:halo] + jnp.dot(bands_hist[gi], hist, preferred_element_type=F32)
            wsums.append(jnp.concatenate([head, wsum[halo:]], axis=0))
        pooled = []
        for gi, win in enumerate(POOL_WINDOWS):
            inv_count = 1.0 / jnp.minimum(win, pos + 1).astype(F32)
            pooled.append((wsums[gi] * inv_count - ugs[gi].astype(F32)).astype(BF16))
        for gi, cols in enumerate(group_cols):
            pool_ref[rows, cols] = jnp.dot(pooled[gi], pw_ref[gi], preferred_element_type=F32) * ps_ref[:, cols]

    for rows in chunks:
        y_ref[rows, :sb_width] = (a_ref[rows, :].astype(F32)
                                  * _silu(g_ref[rows, :sb_width].astype(F32))).astype(BF16)
        y_ref[rows, sb_width:] = (pool_ref[rows, :] * _silu(g_ref[rows, sb_width:].astype(F32))).astype(BF16)
        _outproj_residual(y_ref, wo_ref, ln_ref, x_ref, o_ref, rows)


def _even_tail(a, p, x2d, pool_w, pool_scale, w_out, ln_post, seq, sb_width, cast_weights=()):
    m, d = x2d.shape
    tm, halo = EVEN_TM, POOL_HALO
    steps = m // tm
    slab = lambda wgt: (wgt.shape[0] // steps, wgt.shape[1])
    assert all(wgt.shape[0] % (steps * 16) == 0 for wgt in cast_weights)
    pool_width = d - sb_width
    groups, group_dim, _ = pool_w.shape
    assert seq % tm == 0 and tm % POOL_CHUNK == 0 and POOL_CHUNK % halo == 0 and groups == len(POOL_WINDOWS)
    assert halo >= max(POOL_WINDOWS) - 1 and pool_width == sb_width and d % pool_width == 0
    u_col = (3 * sb_width) // pool_width
    g_col = (3 * sb_width + pool_width) // d
    assert u_col * pool_width == 3 * sb_width and g_col * d == 3 * sb_width + pool_width
    const = lambda i: (0, 0)
    row_block = lambda i: (i, 0)
    return pl.pallas_call(
        functools.partial(_even_tail_kernel, seq=seq, tm=tm, sb_width=sb_width, group_dim=group_dim,
                          n_cast=len(cast_weights)),
        grid=(steps,),
        in_specs=[
            pl.BlockSpec((tm, sb_width), row_block),
            pl.BlockSpec((tm, pool_width), lambda i: (i, u_col)),
            pl.BlockSpec((halo, pool_width), lambda i: (jnp.maximum(i * (tm // halo) - 1, 0), u_col)),
            pl.BlockSpec((tm, d), lambda i: (i, g_col)),
            pl.BlockSpec((tm, d), row_block),
            _resident((groups, group_dim, group_dim), lambda i: (0, 0, 0)),
            _resident((1, pool_width), const),
            _resident((d, d), const),
            _resident((1, d), const),
        ] + [pl.BlockSpec(slab(wgt), row_block) for wgt in cast_weights],
        out_specs=[pl.BlockSpec((tm, d), row_block)] + [pl.BlockSpec(slab(wgt), row_block) for wgt in cast_weights],
        out_shape=([jax.ShapeDtypeStruct((m, d), F32)]
                   + [jax.ShapeDtypeStruct(wgt.shape, BF16) for wgt in cast_weights]),
        scratch_shapes=[pltpu.VMEM((tm, d), BF16), pltpu.VMEM((tm, pool_width), F32)],
        compiler_params=pltpu.CompilerParams(
            dimension_semantics=("parallel",), vmem_limit_bytes=VMEM_LIMIT_BYTES),
        name="even_tail",
    )(a, p, p, p, x2d, pool_w, pool_scale.reshape(1, pool_width), w_out, ln_post.reshape(1, d), *cast_weights)


def _odd_tail_kernel(p_ref, ph_ref, x_ref, sw_ref, dw_ref, db_ref, cg_ref, cb_ref, wo_ref, ln_ref, o_ref,
                     zbuf, dbuf, cbuf, y_ref, *, seq, tm, width):
    i = pl.program_id(0)
    halo = CONV_HALO
    w = width
    planes = w // LANES
    keep = jnp.where((i * tm) % seq == 0, 0.0, 1.0).astype(F32)

    def piece(ref, n, c):
        return ref[:, n * w + c * LANES:n * w + (c + 1) * LANES].astype(F32)

    for c in range(planes):
        zbuf[c, :halo, :] = piece(ph_ref, 2, c) * piece(ph_ref, 0, c) * keep
        zbuf[c, halo:, :] = piece(p_ref, 2, c) * piece(p_ref, 0, c)
        dbuf[c, :halo, :] = piece(ph_ref, 3, c) * _sigmoid(piece(ph_ref, 4, c)) * keep
        dbuf[c, halo:, :] = piece(p_ref, 3, c) * _sigmoid(piece(p_ref, 4, c))

    for ci, r0 in enumerate(range(0, tm, CONV_ROWS)):
        rows = slice(r0, r0 + CONV_ROWS)
        slot = ci % cbuf.shape[0]
        for c in range(planes):
            lanes = slice(c * LANES, (c + 1) * LANES)
            c3 = jnp.zeros((CONV_ROWS, LANES), F32)
            for k in range(SCONV_K):
                off = halo - (SCONV_K - 1) + k + r0
                c3 = c3 + sw_ref[k:k + 1, lanes] * zbuf[c, off:off + CONV_ROWS, :]
            gate_c = _silu(p_ref[rows, 5 * w + c * LANES:5 * w + (c + 1) * LANES].astype(F32))
            bc = p_ref[rows, w + c * LANES:w + (c + 1) * LANES].astype(F32)
            y_ref[rows, lanes] = (bc * c3 * gate_c).astype(BF16)

            acc = jnp.broadcast_to(db_ref[:, lanes], (CONV_ROWS, LANES))
            for k in range(CONF_K):
                off = halo - (CONF_K - 1) + k + r0
                acc = acc + dw_ref[k:k + 1, lanes] * dbuf[c, off:off + CONV_ROWS, :]
            cbuf[slot, :, lanes] = acc

        d = cbuf[slot]
        mu = jnp.mean(d, axis=-1, keepdims=True)
        dc = d - mu
        var = jnp.mean(dc * dc, axis=-1, keepdims=True)
        dn = _silu(dc * lax.rsqrt(var + EPS) * cg_ref[...] + cb_ref[...])
        gate_d = _silu(p_ref[rows, 6 * w:7 * w].astype(F32))
        y_ref[rows, w:] = (dn * gate_d).astype(BF16)
        if (r0 + CONV_ROWS) % OUT_ROWS == 0:
            _outproj_residual(y_ref, wo_ref, ln_ref, x_ref, o_ref, slice(r0 + CONV_ROWS - OUT_ROWS, r0 + CONV_ROWS))


def _odd_tail(p, x2d, sconv_w, dconv_w, dconv_b, cnorm_g, cnorm_b, w_out, ln_post, seq, width):
    m, d = x2d.shape
    n_in = p.shape[1]
    tm, halo = ODD_TM, CONV_HALO
    assert seq % tm == 0 and tm % halo == 0 and tm % OUT_ROWS == 0 and OUT_ROWS % CONV_ROWS == 0
    assert halo >= CONF_K - 1
    assert d == 2 * width and n_in == 7 * width and width % LANES == 0
    assert sconv_w.shape == (SCONV_K, width) and dconv_w.shape == (CONF_K, width)
    const = lambda i: (0, 0)
    return pl.pallas_call(
        functools.partial(_odd_tail_kernel, seq=seq, tm=tm, width=width),
        grid=(m // tm,),
        in_specs=[
            pl.BlockSpec((tm, n_in), lambda i: (i, 0)),
            pl.BlockSpec((halo, n_in), lambda i: (jnp.maximum(i * (tm // halo) - 1, 0), 0)),
            pl.BlockSpec((tm, d), lambda i: (i, 0)),
            _resident((SCONV_K, width), const),
            _resident((CONF_K, width), const),
            _resident((1, width), const),
            _resident((1, width), const),
            _resident((1, width), const),
            _resident((d, d), const),
            _resident((1, d), const),
        ],
        out_specs=pl.BlockSpec((tm, d), lambda i: (i, 0)),
        out_shape=jax.ShapeDtypeStruct((m, d), F32),
        scratch_shapes=[
            pltpu.VMEM((width // LANES, halo + tm, LANES), F32),
            pltpu.VMEM((width // LANES, halo + tm, LANES), F32),
            pltpu.VMEM((2, CONV_ROWS, width), F32),
            pltpu.VMEM((tm, d), BF16),
        ],
        compiler_params=pltpu.CompilerParams(
            dimension_semantics=("parallel",), vmem_limit_bytes=VMEM_LIMIT_BYTES),
        name="odd_tail",
    )(p, p, x2d, sconv_w, dconv_w, dconv_b.reshape(1, width), cnorm_g.reshape(1, width),
      cnorm_b.reshape(1, width), w_out, ln_post.reshape(1, d))


def kernel(x, ln_pre_even, w_in_even, pool_w, pool_scale, w_out_even, ln_post_even,
           ln_pre_odd, w_in_odd, sconv_w, dconv_w, dconv_b, cnorm_g, cnorm_b, w_out_odd, ln_post_odd):
    batch, seq, d = x.shape
    depth = ln_pre_even.shape[0] + ln_pre_odd.shape[0]
    sb_width = d // 2
    heads = sb_width // SB_HEAD_DIM
    conv_width = d // 2
    q_scale = math.log2(math.e) / math.sqrt(SB_HEAD_DIM)

    xf = x.reshape(batch * seq, d)
    odd_weights = None
    for layer in range(depth):
        i = layer // 2
        if layer % 2 == 0:
            n_in = w_in_even.shape[-1]
            col_scale = jnp.where(jnp.arange(n_in) < sb_width, q_scale, 1.0).astype(F32)
            p, w_out, pool_mat = _norm_inproj(
                xf, ln_pre_even[i], w_in_even[i].astype(BF16), col_scale,
                (w_out_even[i], pool_w[i].reshape(-1, pool_w.shape[-1])))
            a = _attention(p, batch, seq, heads)
            coming = (w_in_odd[i], w_out_odd[i]) if layer + 1 < depth else ()
            xf, *rounded = _even_tail(a, p, xf, pool_mat.reshape(pool_w.shape[1:]), pool_scale[i], w_out,
                                      ln_post_even[i], seq, sb_width, coming)
            odd_weights = rounded or None
        else:
            w_in, w_out = odd_weights or (w_in_odd[i].astype(BF16), w_out_odd[i].astype(BF16))
            p, = _norm_inproj(xf, ln_pre_odd[i], w_in)
            xf = _odd_tail(p, xf, sconv_w[i], dconv_w[i], dconv_b[i], cnorm_g[i], cnorm_b[i],
                           w_out, ln_post_odd[i], seq, conv_width)
    return xf.reshape(batch, seq, d)
```
